```python
import jax, jax.numpy as jnp
from jax import lax
import numpy as np

D_MODEL = 1024
BATCH = 2
SEQ = 8192
DEPTH = 4

N_MIXERS = 2
N_LRU_LAYERS = (DEPTH + 1) // 2
N_POOL_LAYERS = DEPTH // 2
D_FF = 11 * D_MODEL // 4
D_RNN = 5 * D_MODEL // 4
LRU_HEADS = 16
LRU_HEAD_DIM = D_RNN // LRU_HEADS
CONV_WIDTH = 4
LRU_C = 8.0
POOL_WINDOWS = (2, 4, 8, 16)
POOL_GROUPS = len(POOL_WINDOWS)
POOL_GROUP_DIM = D_MODEL // POOL_GROUPS
PLE_DIM = 256
RMS_EPS = 1e-6

kernel_name = "hybrid_rglru_pool_macaron_ple"


def rms_norm(x, g):
    xf = x.astype(jnp.float32)
    y = xf * lax.rsqrt(jnp.mean(xf * xf, axis=-1, keepdims=True) + RMS_EPS)
    return (y * g.astype(jnp.float32)).astype(x.dtype)


def swiglu(x, w_gate, w_up, w_down):
    return (jax.nn.silu(x @ w_gate) * (x @ w_up)) @ w_down


def _lin_combine(c1, c2):
    a1, b1 = c1
    a2, b2 = c2
    return a1 * a2, a2 * b1 + b2


def rglru_mixer(x, w_in, conv_w, conv_b, w_a, b_a, w_x, b_x, a_param, w_out):
    B, S, _ = x.shape
    z = x @ w_in
    gate_branch, xb = z[..., :D_RNN], z[..., D_RNN:]
    xp = jnp.pad(xb, ((0, 0), (CONV_WIDTH - 1, 0), (0, 0)))
    xc = conv_b + conv_w[0] * xp[:, 0:S]
    for k in range(1, CONV_WIDTH):
        xc = xc + conv_w[k] * xp[:, k:k + S]
    xh = xc.reshape(B, S, LRU_HEADS, LRU_HEAD_DIM)
    r = jax.nn.sigmoid(jnp.einsum('bshi,hij->bshj', xh, w_a).reshape(B, S, D_RNN) + b_a)
    ig = jax.nn.sigmoid(jnp.einsum('bshi,hij->bshj', xh, w_x).reshape(B, S, D_RNN) + b_x)
    log_a = -LRU_C * r.astype(jnp.float32) * jax.nn.softplus(-a_param.astype(jnp.float32))
    a = jnp.exp(log_a)
    mult = jnp.sqrt(-jnp.expm1(2.0 * log_a))
    bterm = mult * (ig * xc).astype(jnp.float32)
    _, h = lax.associative_scan(_lin_combine, (a, bterm), axis=1)
    y = h.astype(x.dtype) * jax.nn.gelu(gate_branch)
    return y @ w_out


def pool_mixer(x, w, b, scale):
    B, S, _ = x.shape
    xf = x.astype(jnp.float32)
    cs = jnp.cumsum(xf, axis=1)
    t = jnp.arange(S)
    outs = []
    for g, win in enumerate(POOL_WINDOWS):
        lo, hi = g * POOL_GROUP_DIM, (g + 1) * POOL_GROUP_DIM
        c = cs[..., lo:hi]
        prev = jnp.pad(c[:, :S - win], ((0, 0), (win, 0), (0, 0)))
        count = jnp.minimum(t + 1, win).astype(jnp.float32)[None, :, None]
        outs.append((c - prev) / count - xf[..., lo:hi])
    u = jnp.stack(outs, axis=2).astype(x.dtype)
    y = jnp.einsum('bsgc,gcd->bsgd', u, w).reshape(B, S, D_MODEL)
    return (y + b) * scale


def setup_inputs(seed: int = 0) -> dict:
    key = jax.random.key(seed)
    ks = iter(jax.random.split(key, 40))

    def nrm(shape, scale):
        return scale * jax.random.normal(next(ks), shape, jnp.float32)

    D, F, L = D_MODEL, D_FF, DEPTH
    NL, NP = N_LRU_LAYERS, N_POOL_LAYERS
    x = nrm((BATCH, SEQ, D), 1.0)
    p = nrm((DEPTH, BATCH, SEQ, PLE_DIM), 1.0)
    ffn1_norm = 1.0 + nrm((L, D), 0.1)
    ffn1_w_gate = nrm((L, D, F), D ** -0.5)
    ffn1_w_up = nrm((L, D, F), D ** -0.5)
    ffn1_w_down = nrm((L, F, D), F ** -0.5)
    mix_norm = 1.0 + nrm((L, D), 0.1)
    lru_w_in = nrm((NL, D, 2 * D_RNN), D ** -0.5)
    lru_conv_w = nrm((NL, CONV_WIDTH, D_RNN), CONV_WIDTH ** -0.5)
    lru_conv_b = nrm((NL, D_RNN), 0.01)
    lru_w_a = nrm((NL, LRU_HEADS, LRU_HEAD_DIM, LRU_HEAD_DIM), LRU_HEAD_DIM ** -0.5)
    lru_b_a = nrm((NL, D_RNN), 0.01)
    lru_w_x = nrm((NL, LRU_HEADS, LRU_HEAD_DIM, LRU_HEAD_DIM), LRU_HEAD_DIM ** -0.5)
    lru_b_x = nrm((NL, D_RNN), 0.01)
    u = jax.random.uniform(next(ks), (NL, D_RNN), jnp.float32, minval=0.9, maxval=0.999)
    a0 = u ** (1.0 / LRU_C)
    lru_a_param = jnp.log(a0) - jnp.log1p(-a0)
    lru_w_out = nrm((NL, D_RNN, D), D_RNN ** -0.5)
    pool_w = nrm((NP, POOL_GROUPS, POOL_GROUP_DIM, POOL_GROUP_DIM), POOL_GROUP_DIM ** -0.5)
    pool_b = nrm((NP, D), 0.01)
    pool_scale = 1.0 + nrm((NP, D), 0.1)
    ffn2_norm = 1.0 + nrm((L, D), 0.1)
    ffn2_w_gate = nrm((L, D, F), D ** -0.5)
    ffn2_w_up = nrm((L, D, F), D ** -0.5)
    ffn2_w_down = nrm((L, F, D), F ** -0.5)
    ple_norm = 1.0 + nrm((L, D), 0.1)
    ple_w_gate = nrm((L, D, D), D ** -0.5)
    ple_w_proj = nrm((L, PLE_DIM, D), PLE_DIM ** -0.5)
    final_norm = 1.0 + nrm((D,), 0.1)
    return {"x": x, "p": p,
            "ffn1_norm": ffn1_norm, "ffn1_w_gate": ffn1_w_gate, "ffn1_w_up": ffn1_w_up, "ffn1_w_down": ffn1_w_down,
            "mix_norm": mix_norm,
            "lru_w_in": lru_w_in, "lru_conv_w": lru_conv_w, "lru_conv_b": lru_conv_b,
            "lru_w_a": lru_w_a, "lru_b_a": lru_b_a, "lru_w_x": lru_w_x, "lru_b_x": lru_b_x,
            "lru_a_param": lru_a_param, "lru_w_out": lru_w_out,
            "pool_w": pool_w, "pool_b": pool_b, "pool_scale": pool_scale,
            "ffn2_norm": ffn2_norm, "ffn2_w_gate": ffn2_w_gate, "ffn2_w_up": ffn2_w_up, "ffn2_w_down": ffn2_w_down,
            "ple_norm": ple_norm, "ple_w_gate": ple_w_gate, "ple_w_proj": ple_w_proj,
            "final_norm": final_norm}


def reference(x, p, ffn1_norm, ffn1_w_gate, ffn1_w_up, ffn1_w_down, mix_norm,
              lru_w_in, lru_conv_w, lru_conv_b, lru_w_a, lru_b_a, lru_w_x, lru_b_x, lru_a_param, lru_w_out,
              pool_w, pool_b, pool_scale,
              ffn2_norm, ffn2_w_gate, ffn2_w_up, ffn2_w_down,
              ple_norm, ple_w_gate, ple_w_proj, final_norm):
    h = x
    for i in range(DEPTH):
        h = h + 0.5 * swiglu(rms_norm(h, ffn1_norm[i]), ffn1_w_gate[i], ffn1_w_up[i], ffn1_w_down[i])
        hn = rms_norm(h, mix_norm[i])
        j = i // N_MIXERS
        if i % N_MIXERS == 0:
            m = rglru_mixer(hn, lru_w_in[j], lru_conv_w[j], lru_conv_b[j], lru_w_a[j], lru_b_a[j],
                            lru_w_x[j], lru_b_x[j], lru_a_param[j], lru_w_out[j])
        else:
            m = pool_mixer(hn, pool_w[j], pool_b[j], pool_scale[j])
        h = h + m
        h = h + 0.5 * swiglu(rms_norm(h, ffn2_norm[i]), ffn2_w_gate[i], ffn2_w_up[i], ffn2_w_down[i])
        gate = jax.nn.sigmoid(rms_norm(h, ple_norm[i]) @ ple_w_gate[i])
        h = h + gate * (p[i].astype(h.dtype) @ ple_w_proj[i])
    return rms_norm(h, final_norm)
```

```python
import functools

import jax
import jax.numpy as jnp
from jax import lax
from jax.experimental import pallas as pl
from jax.experimental.pallas import tpu as pltpu

D_MODEL = 1024
D_FF = 2816
D_RNN = 1280
LRU_HEADS = 16
LRU_HEAD_DIM = 80
CONV_WIDTH = 4
LRU_C = 8.0
POOL_WINDOWS = (2, 4, 8, 16)
POOL_GROUP_DIM = 256
PLE_DIM = 256
RMS_EPS = 1e-6

SUBLANES = 8
TOKEN_TILE = 512
CONV_HALO = SUBLANES
POOL_HALO = 16
VMEM_LIMIT_BYTES = 56 * 1024 * 1024

F32 = jnp.float32
BF16 = jnp.bfloat16


def _rms(x, g):
    return x * lax.rsqrt(jnp.mean(x * x, axis=-1, keepdims=True) + RMS_EPS) * g


def _mm(a, w):
    return jnp.dot(a, w, preferred_element_type=F32)


def _swiglu_residual(x, g_ref, wg_ref, wu_ref, wd_ref):
    xn = _rms(x, g_ref[...]).astype(BF16)
    gate = _mm(xn, wg_ref[...])
    up = _mm(xn, wu_ref[...])
    act = (gate * jax.nn.sigmoid(gate) * up).astype(BF16)
    return x + 0.5 * _mm(act, wd_ref[...])


def _ffn_kernel(x_ref, g_ref, wg_ref, wu_ref, wd_ref, o_ref):
    o_ref[...] = _swiglu_residual(x_ref[...], g_ref, wg_ref, wu_ref, wd_ref)


def _ffn_ple_kernel(x_ref, p_ref, g_ref, wg_ref, wu_ref, wd_ref, pg_ref, wpg_ref, wpp_ref, fg_ref, o_ref, *,
                    final_norm):
    h = _swiglu_residual(x_ref[...], g_ref, wg_ref, wu_ref, wd_ref)
    gate = jax.nn.sigmoid(_mm(_rms(h, pg_ref[...]).astype(BF16), wpg_ref[...]))
    h = h + gate * _mm(p_ref[...].astype(BF16), wpp_ref[...])
    if final_norm:
        h = _rms(h, fg_ref[...])
    o_ref[...] = h


def _lru_kernel(x_ref, g_ref, win_ref, cw_ref, cb_ref, wa_ref, ba_ref, wx_ref, bx_ref, ap_ref, wout_ref, o_ref,
                xb_buf, a_buf, b_buf, carry_ref):
    tm = x_ref.shape[0]

    @pl.when(pl.program_id(1) == 0)
    def _():
        xb_buf[0:CONV_HALO, :] = jnp.zeros((CONV_HALO, D_RNN), F32)
        carry_ref[...] = jnp.zeros_like(carry_ref)

    x = x_ref[...]
    z = _mm(_rms(x, g_ref[...]).astype(BF16), win_ref[...])
    gate_branch = z[:, :D_RNN]
    xb_buf[CONV_HALO:, :] = z[:, D_RNN:]
    xc = cb_ref[...] + cw_ref[CONV_WIDTH - 1:CONV_WIDTH, :] * xb_buf[CONV_HALO:, :]
    for k in range(CONV_WIDTH - 1):
        off = CONV_HALO - (CONV_WIDTH - 1 - k)
        xc = xc + cw_ref[k:k + 1, :] * xb_buf[off:off + tm, :]
    xb_buf[0:CONV_HALO, :] = xb_buf[tm:tm + CONV_HALO, :]

    xcb = xc.astype(BF16)
    r = jax.nn.sigmoid(_mm(xcb, wa_ref[...]) + ba_ref[...])
    ig = jax.nn.sigmoid(_mm(xcb, wx_ref[...]) + bx_ref[...])
    log_a = (-LRU_C) * r * jax.nn.softplus(-ap_ref[...])
    a = jnp.exp(log_a)
    a_buf[...] = a
    b_buf[...] = jnp.sqrt(-jnp.tanh(log_a) * (1.0 + a * a)) * (ig * xc)

    row = lax.broadcasted_iota(jnp.int32, (SUBLANES, D_RNN), 0)

    def group(gi, carry):
        rows = pl.ds(pl.multiple_of(gi * SUBLANES, SUBLANES), SUBLANES)
        a = a_buf[rows, :]
        b = b_buf[rows, :]
        for s in (1, 2, 4):
            keep = row >= s
            a_prev = jnp.where(keep, pltpu.roll(a, s, axis=0), 1.0)
            b_prev = jnp.where(keep, pltpu.roll(b, s, axis=0), 0.0)
            b = a * b_prev + b
            a = a * a_prev
        h = a * carry + b
        b_buf[rows, :] = h
        return h[SUBLANES - 1:SUBLANES, :]

    carry_ref[...] = lax.fori_loop(0, tm // SUBLANES, group, carry_ref[...])

    y = (b_buf[...] * jax.nn.gelu(gate_branch)).astype(BF16)
    o_ref[...] = x + _mm(y, wout_ref[...])


def _pool_kernel(x_ref, g_ref, w_ref, b_ref, sc_ref, o_ref, hn_buf):
    tm = x_ref.shape[0]
    seq_tile = pl.program_id(1)

    @pl.when(seq_tile == 0)
    def _():
        hn_buf[0:POOL_HALO, :] = jnp.zeros((POOL_HALO, D_MODEL), F32)

    x = x_ref[...]
    hn_buf[POOL_HALO:, :] = _rms(x, g_ref[...])
    t = seq_tile * tm + lax.broadcasted_iota(jnp.int32, (tm, POOL_GROUP_DIM), 0)
    for gi, win in enumerate(POOL_WINDOWS):
        cols = slice(gi * POOL_GROUP_DIM, (gi + 1) * POOL_GROUP_DIM)
        cur = hn_buf[POOL_HALO:, cols]
        acc = cur
        for k in range(1, win):
            acc = acc + hn_buf[POOL_HALO - k:POOL_HALO - k + tm, cols]
        count = jnp.minimum(t + 1, win).astype(F32)
        u = (acc / count - cur).astype(BF16)
        y = _mm(u, w_ref[gi])
        o_ref[:, cols] = x[:, cols] + (y + b_ref[:, cols]) * sc_ref[:, cols]
    hn_buf[0:POOL_HALO, :] = hn_buf[tm:tm + POOL_HALO, :]


def _resident(shape):
    zeros = (0,) * len(shape)
    return pl.BlockSpec(shape, lambda b, s: zeros, pipeline_mode=pl.Buffered(1))


def _tiled_call(kernel, name, n_batch, seq, tiled_inputs, resident_inputs, scratch_shapes=()):
    tiles = seq // TOKEN_TILE
    row_spec = lambda width: pl.BlockSpec((TOKEN_TILE, width), lambda b, s: (b * tiles + s, 0))
    return pl.pallas_call(
        kernel,
        name=name,
        grid=(n_batch, tiles),
        in_specs=[row_spec(a.shape[1]) for a in tiled_inputs] + [_resident(a.shape) for a in resident_inputs],
        out_specs=row_spec(D_MODEL),
        out_shape=jax.ShapeDtypeStruct((n_batch * seq, D_MODEL), F32),
        scratch_shapes=list(scratch_shapes),
        compiler_params=pltpu.CompilerParams(
            dimension_semantics=("arbitrary", "arbitrary"), vmem_limit_bytes=VMEM_LIMIT_BYTES),
    )(*tiled_inputs, *resident_inputs)


def _row(v):
    return v.reshape(1, -1)


def _block_diag(w):
    heads, d, _ = w.shape
    eye = jnp.eye(heads, dtype=w.dtype)
    return jnp.einsum("hij,hk->hikj", w, eye).reshape(heads * d, heads * d)


def kernel(x, p, ffn1_norm, ffn1_w_gate, ffn1_w_up, ffn1_w_down, mix_norm, lru_w_in, lru_conv_w, lru_conv_b, lru_w_a, lru_b_a, lru_w_x, lru_b_x, lru_a_param, lru_w_out, pool_w, pool_b, pool_scale, ffn2_norm, ffn2_w_gate, ffn2_w_up, ffn2_w_down, ple_norm, ple_w_gate, ple_w_proj, final_norm):
    n_batch, seq, d_model = x.shape
    depth = p.shape[0]
    assert d_model == D_MODEL and seq % TOKEN_TILE == 0
    call = functools.partial(_tiled_call, n_batch=n_batch, seq=seq)
    h = x.reshape(n_batch * seq, D_MODEL)
    for i in range(depth):
        h = call(_ffn_kernel, "ffn1", tiled_inputs=[h],
                 resident_inputs=[_row(ffn1_norm[i]), ffn1_w_gate[i].astype(BF16), ffn1_w_up[i].astype(BF16),
                                  ffn1_w_down[i].astype(BF16)])
        j = i // 2
        if i % 2 == 0:
            h = call(_lru_kernel, "lru", tiled_inputs=[h],
                     resident_inputs=[_row(mix_norm[i]), lru_w_in[j].astype(BF16), lru_conv_w[j], _row(lru_conv_b[j]),
                                      _block_diag(lru_w_a[j]).astype(BF16), _row(lru_b_a[j]),
                                      _block_diag(lru_w_x[j]).astype(BF16), _row(lru_b_x[j]),
                                      _row(lru_a_param[j]), lru_w_out[j].astype(BF16)],
                     scratch_shapes=[pltpu.VMEM((TOKEN_TILE + CONV_HALO, D_RNN), F32),
                                     pltpu.VMEM((TOKEN_TILE, D_RNN), F32),
                                     pltpu.VMEM((TOKEN_TILE, D_RNN), F32),
                                     pltpu.VMEM((1, D_RNN), F32)])
        else:
            h = call(_pool_kernel, "pool", tiled_inputs=[h],
                     resident_inputs=[_row(mix_norm[i]), pool_w[j].astype(BF16), _row(pool_b[j]),
                                      _row(pool_scale[j])],
                     scratch_shapes=[pltpu.VMEM((TOKEN_TILE + POOL_HALO, D_MODEL), F32)])
        h = call(functools.partial(_ffn_ple_kernel, final_norm=(i == depth - 1)), "ffn2_ple",
                 tiled_inputs=[h, p[i].reshape(n_batch * seq, PLE_DIM)],
                 resident_inputs=[_row(ffn2_norm[i]), ffn2_w_gate[i].astype(BF16), ffn2_w_up[i].astype(BF16),
                                  ffn2_w_down[i].astype(BF16), _row(ple_norm[i]), ple_w_gate[i].astype(BF16),
                                  ple_w_proj[i].astype(BF16), _row(final_norm)])
    return h.reshape(n_batch, seq, D_MODEL)
```

```python
import functools

import jax
import jax.numpy as jnp
from jax import lax
from jax.experimental import pallas as pl
from jax.experimental.pallas import tpu as pltpu

D_MODEL = 1024
D_FF = 2816
D_RNN = 1280
LRU_HEADS = 16
LRU_HEAD_DIM = 80
CONV_WIDTH = 4
LRU_C = 8.0
POOL_WINDOWS = (2, 4, 8, 16)
POOL_GROUP_DIM = 256
PLE_DIM = 256
RMS_EPS = 1e-6

SUBLANES = 8
LANES = 128
GATE_CHUNK = 256
GATE_WINDOW = 512
TOKEN_TILE = 512
CONV_HALO = SUBLANES
POOL_HALO = 16
VMEM_LIMIT_BYTES = 56 * 1024 * 1024

F32 = jnp.float32
BF16 = jnp.bfloat16


def _rms(x, g):
    return x * lax.rsqrt(jnp.mean(x * x, axis=-1, keepdims=True) + RMS_EPS) * g


def _mm(a, w):
    return jnp.dot(a, w, preferred_element_type=F32)


def _swiglu_residual(x, g_ref, wg_ref, wu_ref, wd_ref):
    xn = _rms(x, g_ref[...]).astype(BF16)
    gate = _mm(xn, wg_ref[...])
    up = _mm(xn, wu_ref[...])
    act = (gate * jax.nn.sigmoid(gate) * up).astype(BF16)
    return x + 0.5 * _mm(act, wd_ref[...])


def _ffn_kernel(x_ref, g_ref, wg_ref, wu_ref, wd_ref, o_ref):
    o_ref[...] = _swiglu_residual(x_ref[...], g_ref, wg_ref, wu_ref, wd_ref)


def _ffn_ple_kernel(x_ref, p_ref, g_ref, wg_ref, wu_ref, wd_ref, pg_ref, wpg_ref, wpp_ref, fg_ref, o_ref, *,
                    final_norm):
    h = _swiglu_residual(x_ref[...], g_ref, wg_ref, wu_ref, wd_ref)
    gate = jax.nn.sigmoid(_mm(_rms(h, pg_ref[...]).astype(BF16), wpg_ref[...]))
    h = h + gate * _mm(p_ref[...].astype(BF16), wpp_ref[...])
    if final_norm:
        h = _rms(h, fg_ref[...])
    o_ref[...] = h


def _gate_window_start(chunk):
    lo = chunk * GATE_CHUNK // LRU_HEAD_DIM * LRU_HEAD_DIM
    hi = -(-(chunk + 1) * GATE_CHUNK // LRU_HEAD_DIM) * LRU_HEAD_DIM
    start = min(lo // LANES * LANES, D_RNN - GATE_WINDOW)
    assert start <= lo and hi <= start + GATE_WINDOW
    return start


def _block_diag_mm(a, w_ref):
    out = []
    for chunk in range(D_RNN // GATE_CHUNK):
        k0 = _gate_window_start(chunk)
        cols = slice(chunk * GATE_CHUNK, (chunk + 1) * GATE_CHUNK)
        out.append(_mm(a[:, k0:k0 + GATE_WINDOW], w_ref[k0:k0 + GATE_WINDOW, cols]))
    return jnp.concatenate(out, axis=1)


def _lru_kernel(x_ref, g_ref, win_ref, cw_ref, cb_ref, wa_ref, ba_ref, wx_ref, bx_ref, ap_ref, wout_ref, o_ref,
                xb_buf, a_buf, b_buf, carry_ref):
    tm = x_ref.shape[0]

    @pl.when(pl.program_id(1) == 0)
    def _():
        xb_buf[0:CONV_HALO, :] = jnp.zeros((CONV_HALO, D_RNN), F32)
        carry_ref[...] = jnp.zeros_like(carry_ref)

    x = x_ref[...]
    z = _mm(_rms(x, g_ref[...]).astype(BF16), win_ref[...])
    gate_branch = z[:, :D_RNN]
    xb_buf[CONV_HALO:, :] = z[:, D_RNN:]
    xc = cb_ref[...] + cw_ref[CONV_WIDTH - 1:CONV_WIDTH, :] * xb_buf[CONV_HALO:, :]
    for k in range(CONV_WIDTH - 1):
        off = CONV_HALO - (CONV_WIDTH - 1 - k)
        xc = xc + cw_ref[k:k + 1, :] * xb_buf[off:off + tm, :]
    xb_buf[0:CONV_HALO, :] = xb_buf[tm:tm + CONV_HALO, :]

    xcb = xc.astype(BF16)
    r = jax.nn.sigmoid(_block_diag_mm(xcb, wa_ref) + ba_ref[...])
    ig = jax.nn.sigmoid(_block_diag_mm(xcb, wx_ref) + bx_ref[...])
    log_a = (-LRU_C) * r * jax.nn.softplus(-ap_ref[...])
    a = jnp.exp(log_a)
    a_buf[...] = a
    b_buf[...] = jnp.sqrt(-jnp.tanh(log_a) * (1.0 + a * a)) * (ig * xc)

    row = lax.broadcasted_iota(jnp.int32, (SUBLANES, D_RNN), 0)

    def group(gi, carry):
        rows = pl.ds(pl.multiple_of(gi * SUBLANES, SUBLANES), SUBLANES)
        a = a_buf[rows, :]
        b = b_buf[rows, :]
        for s in (1, 2, 4):
            keep = row >= s
            a_prev = jnp.where(keep, pltpu.roll(a, s, axis=0), 1.0)
            b_prev = jnp.where(keep, pltpu.roll(b, s, axis=0), 0.0)
            b = a * b_prev + b
            a = a * a_prev
        h = a * carry + b
        b_buf[rows, :] = h
        return h[SUBLANES - 1:SUBLANES, :]

    carry_ref[...] = lax.fori_loop(0, tm // SUBLANES, group, carry_ref[...])

    y = (b_buf[...] * jax.nn.gelu(gate_branch)).astype(BF16)
    o_ref[...] = x + _mm(y, wout_ref[...])


def _pool_kernel(x_ref, g_ref, w_ref, b_ref, sc_ref, o_ref, hn_buf):
    tm = x_ref.shape[0]
    seq_tile = pl.program_id(1)

    @pl.when(seq_tile == 0)
    def _():
        hn_buf[0:POOL_HALO, :] = jnp.zeros((POOL_HALO, D_MODEL), F32)

    x = x_ref[...]
    hn_buf[POOL_HALO:, :] = _rms(x, g_ref[...])
    t = seq_tile * tm + lax.broadcasted_iota(jnp.int32, (tm, POOL_GROUP_DIM), 0)
    for gi, win in enumerate(POOL_WINDOWS):
        cols = slice(gi * POOL_GROUP_DIM, (gi + 1) * POOL_GROUP_DIM)
        cur = hn_buf[POOL_HALO:, cols]
        acc = cur
        for k in range(1, win):
            acc = acc + hn_buf[POOL_HALO - k:POOL_HALO - k + tm, cols]
        count = jnp.minimum(t + 1, win).astype(F32)
        u = (acc / count - cur).astype(BF16)
        y = _mm(u, w_ref[gi])
        o_ref[:, cols] = x[:, cols] + (y + b_ref[:, cols]) * sc_ref[:, cols]
    hn_buf[0:POOL_HALO, :] = hn_buf[tm:tm + POOL_HALO, :]


def _layer_spec(stacked, layer):
    zeros = (0,) * (stacked.ndim - 1)
    return pl.BlockSpec((None,) + stacked.shape[1:], lambda b, s: (layer,) + zeros, pipeline_mode=pl.Buffered(1))


def _tiled_call(kernel, name, n_batch, seq, tiled_inputs, layer_inputs, scratch_shapes=()):
    tiles = seq // TOKEN_TILE

    def row_spec(width, copy=0):
        base = copy * n_batch * tiles
        return pl.BlockSpec((TOKEN_TILE, width), lambda b, s: (base + b * tiles + s, 0))

    return pl.pallas_call(
        kernel,
        name=name,
        grid=(n_batch, tiles),
        in_specs=[row_spec(a.shape[1], k) for a, k in tiled_inputs] + [_layer_spec(a, l) for a, l in layer_inputs],
        out_specs=row_spec(D_MODEL),
        out_shape=jax.ShapeDtypeStruct((n_batch * seq, D_MODEL), F32),
        scratch_shapes=list(scratch_shapes),
        compiler_params=pltpu.CompilerParams(
            dimension_semantics=("arbitrary", "arbitrary"), vmem_limit_bytes=VMEM_LIMIT_BYTES),
    )(*[a for a, _ in tiled_inputs], *[a for a, _ in layer_inputs])


def _rows_of(stacked):
    return stacked.reshape(stacked.shape[0], 1, stacked.shape[1])


def _block_diag(w):
    layers, heads, d, _ = w.shape
    eye = jnp.eye(heads, dtype=w.dtype)
    return (w[:, :, :, None, :] * eye[None, :, None, :, None]).reshape(layers, heads * d, heads * d)


def kernel(x, p, ffn1_norm, ffn1_w_gate, ffn1_w_up, ffn1_w_down, mix_norm, lru_w_in, lru_conv_w, lru_conv_b, lru_w_a, lru_b_a, lru_w_x, lru_b_x, lru_a_param, lru_w_out, pool_w, pool_b, pool_scale, ffn2_norm, ffn2_w_gate, ffn2_w_up, ffn2_w_down, ple_norm, ple_w_gate, ple_w_proj, final_norm):
    n_batch, seq, d_model = x.shape
    depth = p.shape[0]
    assert d_model == D_MODEL and seq % TOKEN_TILE == 0
    call = functools.partial(_tiled_call, n_batch=n_batch, seq=seq)
    bf = lambda w: w.astype(BF16)
    ffn1 = [_rows_of(ffn1_norm), bf(ffn1_w_gate), bf(ffn1_w_up), bf(ffn1_w_down)]
    ffn2 = [_rows_of(ffn2_norm), bf(ffn2_w_gate), bf(ffn2_w_up), bf(ffn2_w_down),
            _rows_of(ple_norm), bf(ple_w_gate), bf(ple_w_proj)]
    mix_g = _rows_of(mix_norm)
    lru = [bf(lru_w_in), lru_conv_w, _rows_of(lru_conv_b), bf(_block_diag(lru_w_a)), _rows_of(lru_b_a),
           bf(_block_diag(lru_w_x)), _rows_of(lru_b_x), _rows_of(lru_a_param), bf(lru_w_out)]
    pool = [bf(pool_w), _rows_of(pool_b), _rows_of(pool_scale)]
    final_g = final_norm.reshape(1, 1, D_MODEL)
    p_rows = p.reshape(depth * n_batch * seq, PLE_DIM)

    h = x.reshape(n_batch * seq, D_MODEL)
    for i in range(depth):
        h = call(_ffn_kernel, "ffn1", tiled_inputs=[(h, 0)], layer_inputs=[(a, i) for a in ffn1])
        j = i // 2
        if i % 2 == 0:
            h = call(_lru_kernel, "lru", tiled_inputs=[(h, 0)],
                     layer_inputs=[(mix_g, i)] + [(a, j) for a in lru],
                     scratch_shapes=[pltpu.VMEM((TOKEN_TILE + CONV_HALO, D_RNN), F32),
                                     pltpu.VMEM((TOKEN_TILE, D_RNN), F32),
                                     pltpu.VMEM((TOKEN_TILE, D_RNN), F32),
                                     pltpu.VMEM((1, D_RNN), F32)])
        else:
            h = call(_pool_kernel, "pool", tiled_inputs=[(h, 0)],
                     layer_inputs=[(mix_g, i)] + [(a, j) for a in pool],
                     scratch_shapes=[pltpu.VMEM((TOKEN_TILE + POOL_HALO, D_MODEL), F32)])
        h = call(functools.partial(_ffn_ple_kernel, final_norm=(i == depth - 1)), "ffn2_ple",
                 tiled_inputs=[(h, 0), (p_rows, i)],
                 layer_inputs=[(a, i) for a in ffn2] + [(final_g, 0)])
    return h.reshape(n_batch, seq, D_MODEL)
```

```python
import functools

import jax
import jax.numpy as jnp
from jax import lax
from jax.experimental import pallas as pl
from jax.experimental.pallas import tpu as pltpu

D_MODEL = 1024
D_FF = 2816
D_RNN = 1280
LRU_HEADS = 16
LRU_HEAD_DIM = 80
CONV_WIDTH = 4
LRU_C = 8.0
POOL_WINDOWS = (2, 4, 8, 16)
POOL_GROUP_DIM = 256
PLE_DIM = 256
RMS_EPS = 1e-6

SUBLANES = 8
LANES = 128
GATE_CHUNK = 256
GATE_WINDOW = 512
FF_CHUNK = 256
MIX_TILE = 512
CHUNK = MIX_TILE // SUBLANES
FFN_TILE = 1024
CONV_BOUNDARY = CONV_WIDTH - 1
POOL_BOUNDARY = max(POOL_WINDOWS) - 1
VMEM_LIMIT_BYTES = 56 * 1024 * 1024

F32 = jnp.float32
BF16 = jnp.bfloat16


def _rms(x, g):
    return x * lax.rsqrt(jnp.mean(x * x, axis=-1, keepdims=True) + RMS_EPS) * g


def _mm(a, w):
    return jnp.dot(a, w, preferred_element_type=F32)


def _interleave(x):
    w = x.shape[1]
    return pltpu.einshape("cjd->jcd", x.reshape(SUBLANES, CHUNK, w)).reshape(MIX_TILE, w)


def _deinterleave(x):
    w = x.shape[1]
    return pltpu.einshape("jcd->cjd", x.reshape(CHUNK, SUBLANES, w)).reshape(MIX_TILE, w)


def _pos(x, j):
    return x[j * SUBLANES:(j + 1) * SUBLANES, :]


def _from_previous_chunk(cur, prev_tile):
    chunk = lax.broadcasted_iota(jnp.int32, cur.shape, 0)
    return pltpu.roll(jnp.where(chunk == SUBLANES - 1, prev_tile, cur), 1, axis=0)


def _fill_boundary(ext_ref, tail_ref, cur, n):
    for i in range(n):
        rows = slice(i * SUBLANES, (i + 1) * SUBLANES)
        ext_ref[rows, :] = _from_previous_chunk(_pos(cur, CHUNK - n + i), tail_ref[rows, :])
    ext_ref[n * SUBLANES:, :] = cur
    tail_ref[...] = cur[(CHUNK - n) * SUBLANES:, :]


def _swiglu_residual(x, g_ref, wg_ref, wu_ref, wd_ref):
    xn = _rms(x, g_ref[...]).astype(BF16)
    y = None
    for c in range(D_FF // FF_CHUNK):
        cols = slice(c * FF_CHUNK, (c + 1) * FF_CHUNK)
        gate = _mm(xn, wg_ref[:, cols])
        up = _mm(xn, wu_ref[:, cols])
        act = (gate * jax.nn.sigmoid(gate) * up).astype(BF16)
        part = _mm(act, wd_ref[cols, :])
        y = part if y is None else y + part
    return x + 0.5 * y


def _ffn_kernel(x_ref, g_ref, wg_ref, wu_ref, wd_ref, o_ref, *, interleave_input):
    for r in range(x_ref.shape[0] // MIX_TILE):
        rows = slice(r * MIX_TILE, (r + 1) * MIX_TILE)
        x = x_ref[rows, :]
        if interleave_input:
            x = _interleave(x)
        o_ref[rows, :] = _swiglu_residual(x, g_ref, wg_ref, wu_ref, wd_ref)


def _ffn_ple_kernel(x_ref, p_ref, g_ref, wg_ref, wu_ref, wd_ref, pg_ref, wpg_ref, wpp_ref, fg_ref, o_ref, *,
                    final_norm):
    for r in range(x_ref.shape[0] // MIX_TILE):
        rows = slice(r * MIX_TILE, (r + 1) * MIX_TILE)
        h = _swiglu_residual(x_ref[rows, :], g_ref, wg_ref, wu_ref, wd_ref)
        gate = jax.nn.sigmoid(_mm(_rms(h, pg_ref[...]).astype(BF16), wpg_ref[...]))
        h = h + gate * _mm(_interleave(p_ref[rows, :]).astype(BF16), wpp_ref[...])
        if final_norm:
            h = _deinterleave(_rms(h, fg_ref[...]))
        o_ref[rows, :] = h


def _gate_window_start(chunk):
    lo = chunk * GATE_CHUNK // LRU_HEAD_DIM * LRU_HEAD_DIM
    hi = -(-(chunk + 1) * GATE_CHUNK // LRU_HEAD_DIM) * LRU_HEAD_DIM
    start = min(lo // LANES * LANES, D_RNN - GATE_WINDOW)
    assert start <= lo and hi <= start + GATE_WINDOW
    return start


def _block_diag_mm(a, w_ref):
    out = []
    for chunk in range(D_RNN // GATE_CHUNK):
        k0 = _gate_window_start(chunk)
        cols = slice(chunk * GATE_CHUNK, (chunk + 1) * GATE_CHUNK)
        out.append(_mm(a[:, k0:k0 + GATE_WINDOW], w_ref[k0:k0 + GATE_WINDOW, cols]))
    return jnp.concatenate(out, axis=1)


def _scan_chunks(decay, inp, state):
    chunk = lax.broadcasted_iota(jnp.int32, decay.shape, 0)
    inp = inp + jnp.where(chunk == 0, decay * state, 0.0)
    step = 1
    while step < SUBLANES:
        seen = chunk >= step
        inp = decay * jnp.where(seen, pltpu.roll(inp, step, axis=0), 0.0) + inp
        decay = decay * jnp.where(seen, pltpu.roll(decay, step, axis=0), 1.0)
        step *= 2
    return inp


def _lru_kernel(x_ref, g_ref, win_ref, cw_ref, cb_ref, wa_ref, ba_ref, wx_ref, bx_ref, ap_ref, wout_ref, o_ref,
                xb_ext, xb_tail, a_buf, b_buf, state_ref):
    @pl.when(pl.program_id(1) == 0)
    def _():
        xb_tail[...] = jnp.zeros_like(xb_tail)
        state_ref[...] = jnp.zeros_like(state_ref)

    x = x_ref[...]
    z = _mm(_rms(x, g_ref[...]).astype(BF16), win_ref[...])
    gate_branch = z[:, :D_RNN]
    xb = z[:, D_RNN:]
    _fill_boundary(xb_ext, xb_tail, xb, CONV_BOUNDARY)
    xc = cb_ref[...] + cw_ref[CONV_WIDTH - 1:CONV_WIDTH, :] * xb
    for m in range(1, CONV_WIDTH):
        start = (CONV_BOUNDARY - m) * SUBLANES
        xc = xc + cw_ref[CONV_WIDTH - 1 - m:CONV_WIDTH - m, :] * xb_ext[start:start + MIX_TILE, :]

    xcb = xc.astype(BF16)
    r = jax.nn.sigmoid(_block_diag_mm(xcb, wa_ref) + ba_ref[...])
    ig = jax.nn.sigmoid(_block_diag_mm(xcb, wx_ref) + bx_ref[...])
    log_a = (-LRU_C) * r * jax.nn.softplus(-ap_ref[...])
    a = jnp.exp(log_a)
    a_buf[...] = a
    v = -jnp.tanh(log_a) * (1.0 + a * a)
    b_buf[...] = jnp.where(v > 0.0, v * lax.rsqrt(v), 0.0) * (ig * xc)

    local = _pos(b_buf, 0)
    decay = _pos(a_buf, 0)
    for j in range(1, CHUNK):
        rows = slice(j * SUBLANES, (j + 1) * SUBLANES)
        a_j = a_buf[rows, :]
        local = a_j * local + b_buf[rows, :]
        decay = a_j * decay
        b_buf[rows, :] = local
        a_buf[rows, :] = decay
    state = state_ref[...]
    chunk_end = _scan_chunks(decay, local, state)
    state_ref[...] = jnp.broadcast_to(chunk_end[SUBLANES - 1:, :], state.shape)
    chunk_entry = _from_previous_chunk(chunk_end, state)
    h = (b_buf[...].reshape(CHUNK, SUBLANES, D_RNN)
         + a_buf[...].reshape(CHUNK, SUBLANES, D_RNN) * chunk_entry[None]).reshape(MIX_TILE, D_RNN)

    y = (h * jax.nn.gelu(gate_branch)).astype(BF16)
    o_ref[...] = x + _mm(y, wout_ref[...])


def _window_sum(ext, win):
    cur = ext[(POOL_BOUNDARY - (win - 1)) * SUBLANES:, :]
    step = 1
    while step < win:
        cur = cur[step * SUBLANES:, :] + cur[:-step * SUBLANES, :]
        step *= 2
    return cur


def _pool_kernel(x_ref, g_ref, w_ref, b_ref, sc_ref, o_ref, hn_ext, hn_tail):
    seq_tile = pl.program_id(1)

    @pl.when(seq_tile == 0)
    def _():
        hn_tail[...] = jnp.zeros_like(hn_tail)

    x = x_ref[...]
    hn = _rms(x, g_ref[...])
    _fill_boundary(hn_ext, hn_tail, hn, POOL_BOUNDARY)
    row = lax.broadcasted_iota(jnp.int32, (MIX_TILE, POOL_GROUP_DIM), 0)
    t = seq_tile * MIX_TILE + (row & (SUBLANES - 1)) * CHUNK + lax.shift_right_logical(row, SUBLANES.bit_length() - 1)
    for gi, win in enumerate(POOL_WINDOWS):
        cols = slice(gi * POOL_GROUP_DIM, (gi + 1) * POOL_GROUP_DIM)
        count = jnp.minimum(t + 1, win).astype(F32)
        u = (_window_sum(hn_ext[:, cols], win) / count - hn[:, cols]).astype(BF16)
        y = _mm(u, w_ref[gi])
        o_ref[:, cols] = x[:, cols] + (y + b_ref[:, cols]) * sc_ref[:, cols]


def _layer_spec(stacked, layer):
    zeros = (0,) * (stacked.ndim - 1)
    return pl.BlockSpec((None,) + stacked.shape[1:], lambda b, s: (layer,) + zeros, pipeline_mode=pl.Buffered(1))


def _tiled_call(kernel, name, tile, n_batch, seq, tiled_inputs, layer_inputs, scratch_shapes=()):
    tiles = seq // tile

    def row_spec(width, copy=0):
        base = copy * n_batch * tiles
        return pl.BlockSpec((tile, width), lambda b, s: (base + b * tiles + s, 0))

    return pl.pallas_call(
        kernel,
        name=name,
        grid=(n_batch, tiles),
        in_specs=[row_spec(a.shape[1], k) for a, k in tiled_inputs] + [_layer_spec(a, l) for a, l in layer_inputs],
        out_specs=row_spec(D_MODEL),
        out_shape=jax.ShapeDtypeStruct((n_batch * seq, D_MODEL), F32),
        scratch_shapes=list(scratch_shapes),
        compiler_params=pltpu.CompilerParams(
            dimension_semantics=("arbitrary", "arbitrary"), vmem_limit_bytes=VMEM_LIMIT_BYTES),
    )(*[a for a, _ in tiled_inputs], *[a for a, _ in layer_inputs])


def _rows_of(stacked):
    return stacked.reshape(stacked.shape[0], 1, stacked.shape[1])


def _block_diag(w):
    layers, heads, d, _ = w.shape
    n = heads * d
    col = jnp.arange(n)
    spread = (col[None, :] % d == jnp.arange(d)[:, None]).astype(BF16)
    tiled = jnp.einsum("lrj,jc->lrc", w.reshape(layers, n, d).astype(BF16), spread, preferred_element_type=F32)
    same_head = (col[:, None] // d) == (col[None, :] // d)
    return jnp.where(same_head[None], tiled, 0.0).astype(BF16)


def kernel(x, p, ffn1_norm, ffn1_w_gate, ffn1_w_up, ffn1_w_down, mix_norm, lru_w_in, lru_conv_w, lru_conv_b, lru_w_a, lru_b_a, lru_w_x, lru_b_x, lru_a_param, lru_w_out, pool_w, pool_b, pool_scale, ffn2_norm, ffn2_w_gate, ffn2_w_up, ffn2_w_down, ple_norm, ple_w_gate, ple_w_proj, final_norm):
    n_batch, seq, d_model = x.shape
    depth = p.shape[0]
    assert d_model == D_MODEL and seq % FFN_TILE == 0 and FFN_TILE % MIX_TILE == 0
    call = functools.partial(_tiled_call, n_batch=n_batch, seq=seq)
    bf = lambda w: w.astype(BF16)
    ffn1 = [_rows_of(ffn1_norm), bf(ffn1_w_gate), bf(ffn1_w_up), bf(ffn1_w_down)]
    ffn2 = [_rows_of(ffn2_norm), bf(ffn2_w_gate), bf(ffn2_w_up), bf(ffn2_w_down),
            _rows_of(ple_norm), bf(ple_w_gate), bf(ple_w_proj)]
    mix_g = _rows_of(mix_norm)
    lru = [bf(lru_w_in), lru_conv_w, _rows_of(lru_conv_b), _block_diag(lru_w_a), _rows_of(lru_b_a),
           _block_diag(lru_w_x), _rows_of(lru_b_x), _rows_of(lru_a_param), bf(lru_w_out)]
    pool = [bf(pool_w), _rows_of(pool_b), _rows_of(pool_scale)]
    final_g = final_norm.reshape(1, 1, D_MODEL)
    p_rows = p.reshape(depth * n_batch * seq, PLE_DIM)

    h = x.reshape(n_batch * seq, D_MODEL)
    for i in range(depth):
        h = call(functools.partial(_ffn_kernel, interleave_input=(i == 0)), "ffn1", FFN_TILE,
                 tiled_inputs=[(h, 0)], layer_inputs=[(a, i) for a in ffn1])
        j = i // 2
        if i % 2 == 0:
            h = call(_lru_kernel, "lru", MIX_TILE, tiled_inputs=[(h, 0)],
                     layer_inputs=[(mix_g, i)] + [(a, j) for a in lru],
                     scratch_shapes=[pltpu.VMEM((CONV_BOUNDARY * SUBLANES + MIX_TILE, D_RNN), F32),
                                     pltpu.VMEM((CONV_BOUNDARY * SUBLANES, D_RNN), F32),
                                     pltpu.VMEM((MIX_TILE, D_RNN), F32),
                                     pltpu.VMEM((MIX_TILE, D_RNN), F32),
                                     pltpu.VMEM((SUBLANES, D_RNN), F32)])
        else:
            h = call(_pool_kernel, "pool", MIX_TILE, tiled_inputs=[(h, 0)],
                     layer_inputs=[(mix_g, i)] + [(a, j) for a in pool],
                     scratch_shapes=[pltpu.VMEM((POOL_BOUNDARY * SUBLANES + MIX_TILE, D_MODEL), F32),
                                     pltpu.VMEM((POOL_BOUNDARY * SUBLANES, D_MODEL), F32)])
        h = call(functools.partial(_ffn_ple_kernel, final_norm=(i == depth - 1)), "ffn2_ple", FFN_TILE,
                 tiled_inputs=[(h, 0), (p_rows, i)],
                 layer_inputs=[(a, i) for a in ffn2] + [(final_g, 0)])
    return h.reshape(n_batch, seq, D_MODEL)
```

```python
import functools

import jax
import jax.numpy as jnp
from jax import lax
from jax.experimental import pallas as pl
from jax.experimental.pallas import tpu as pltpu

D_MODEL = 1024
D_FF = 2816
D_RNN = 1280
LRU_HEADS = 16
LRU_HEAD_DIM = 80
CONV_WIDTH = 4
LRU_C = 8.0
POOL_WINDOWS = (2, 4, 8, 16)
POOL_GROUP_DIM = 256
PLE_DIM = 256
RMS_EPS = 1e-6

SUBLANES = 8
LANES = 128
BF16_SUBLANES = 16
GATE_CHUNK = 256
GATE_WINDOW = 512
FF_CHUNK = 256
MIX_TILE = 512
CHUNK = MIX_TILE // SUBLANES
GRID_TILE = 1024
CONV_BOUNDARY = CONV_WIDTH - 1
POOL_BOUNDARY = max(POOL_WINDOWS) - 1
VMEM_LIMIT_BYTES = 56 * 1024 * 1024

F32 = jnp.float32
BF16 = jnp.bfloat16


def _rms(x, g):
    return x * lax.rsqrt(jnp.mean(x * x, axis=-1, keepdims=True) + RMS_EPS) * g


def _mm(a, w):
    return jnp.dot(a, w, preferred_element_type=F32)


def _interleave(x):
    w = x.shape[1]
    return jnp.swapaxes(x.reshape(SUBLANES, CHUNK, w), 0, 1).reshape(MIX_TILE, w)


def _deinterleave(x):
    w = x.shape[1]
    return jnp.swapaxes(x.reshape(CHUNK, SUBLANES, w), 0, 1).reshape(MIX_TILE, w)


def _pos(x, j):
    return x[j * SUBLANES:(j + 1) * SUBLANES, :]


def _from_previous_chunk(cur, prev_tile):
    chunk = lax.broadcasted_iota(jnp.int32, cur.shape, 0)
    return pltpu.roll(jnp.where(chunk == SUBLANES - 1, prev_tile, cur), 1, axis=0)


def _fill_boundary(ext_ref, prev_tail, cur, n):
    for i in range(n):
        ext_ref[i * SUBLANES:(i + 1) * SUBLANES, :] = _from_previous_chunk(_pos(cur, CHUNK - n + i), _pos(prev_tail, i))
    ext_ref[n * SUBLANES:, :] = cur
    return cur[(CHUNK - n) * SUBLANES:, :]


def _swiglu_residual(x, g_ref, wg_ref, wu_ref, wd_ref):
    xn = _rms(x, g_ref[...]).astype(BF16)
    y = None
    for c in range(D_FF // FF_CHUNK):
        cols = slice(c * FF_CHUNK, (c + 1) * FF_CHUNK)
        gate = _mm(xn, wg_ref[:, cols])
        up = _mm(xn, wu_ref[:, cols])
        act = (gate * jax.nn.sigmoid(gate) * up).astype(BF16)
        part = _mm(act, wd_ref[cols, :])
        y = part if y is None else y + part
    return x + 0.5 * y


def _ffn_kernel(x_ref, g_ref, wg_ref, wu_ref, wd_ref, o_ref, *, interleave_input):
    for r in range(x_ref.shape[0] // MIX_TILE):
        rows = slice(r * MIX_TILE, (r + 1) * MIX_TILE)
        x = x_ref[rows, :]
        if interleave_input:
            x = _interleave(x)
        o_ref[rows, :] = _swiglu_residual(x, g_ref, wg_ref, wu_ref, wd_ref)


def _ffn_ple_kernel(x_ref, p_ref, g_ref, wg_ref, wu_ref, wd_ref, pg_ref, wpg_ref, wpp_ref, fg_ref, o_ref, *,
                    final_norm):
    for r in range(x_ref.shape[0] // MIX_TILE):
        rows = slice(r * MIX_TILE, (r + 1) * MIX_TILE)
        h = _swiglu_residual(x_ref[rows, :], g_ref, wg_ref, wu_ref, wd_ref)
        gate = jax.nn.sigmoid(_mm(_rms(h, pg_ref[...]).astype(BF16), wpg_ref[...]))
        h = h + gate * _mm(_interleave(p_ref[rows, :]).astype(BF16), wpp_ref[...])
        if final_norm:
            h = _deinterleave(_rms(h, fg_ref[...]))
        o_ref[rows, :] = h


def _gate_window_start(chunk):
    lo = chunk * GATE_CHUNK // LRU_HEAD_DIM * LRU_HEAD_DIM
    hi = -(-(chunk + 1) * GATE_CHUNK // LRU_HEAD_DIM) * LRU_HEAD_DIM
    start = min(lo // LANES * LANES, D_RNN - GATE_WINDOW)
    assert start <= lo and hi <= start + GATE_WINDOW
    return start


def _block_diag_mm(a, w_ref):
    out = []
    for chunk in range(D_RNN // GATE_CHUNK):
        k0 = _gate_window_start(chunk)
        cols = slice(chunk * GATE_CHUNK, (chunk + 1) * GATE_CHUNK)
        out.append(_mm(a[:, k0:k0 + GATE_WINDOW], w_ref[k0:k0 + GATE_WINDOW, cols]))
    return jnp.concatenate(out, axis=1)


def _scan_chunks(decay, inp, state):
    chunk = lax.broadcasted_iota(jnp.int32, decay.shape, 0)
    inp = inp + jnp.where(chunk == 0, decay * state, 0.0)
    step = 1
    while step < SUBLANES:
        seen = chunk >= step
        inp = decay * jnp.where(seen, pltpu.roll(inp, step, axis=0), 0.0) + inp
        decay = decay * jnp.where(seen, pltpu.roll(decay, step, axis=0), 1.0)
        step *= 2
    return inp


def _lru_project(x, xb_tail, g_ref, win_ref, cw_ref, cb_ref, wa_ref, wx_ref, xb_ext, a_buf, b_buf, xc_buf, gate_buf):
    xn = _rms(x, g_ref[...]).astype(BF16)
    xb = _mm(xn, win_ref[:, D_RNN:])
    gate_buf[...] = jax.nn.gelu(_mm(xn, win_ref[:, :D_RNN]))
    xb_tail = _fill_boundary(xb_ext, xb_tail, xb, CONV_BOUNDARY)
    xc = cb_ref[...] + cw_ref[CONV_WIDTH - 1:CONV_WIDTH, :] * xb
    for m in range(1, CONV_WIDTH):
        start = (CONV_BOUNDARY - m) * SUBLANES
        xc = xc + cw_ref[CONV_WIDTH - 1 - m:CONV_WIDTH - m, :] * xb_ext[start:start + MIX_TILE, :]
    xc_buf[...] = xc
    xcb = xc.astype(BF16)
    a_buf[...] = _block_diag_mm(xcb, wa_ref)
    b_buf[...] = _block_diag_mm(xcb, wx_ref)
    return xb_tail


def _lru_recur(x, state, ba_ref, bx_ref, ap_ref, wout_ref, a_buf, b_buf, xc_buf, gate_buf):
    r = jax.nn.sigmoid(a_buf[...] + ba_ref[...])
    ig = jax.nn.sigmoid(b_buf[...] + bx_ref[...])
    log_a = (-LRU_C) * r * jax.nn.softplus(-ap_ref[...])
    a = jnp.exp(log_a)
    a_buf[...] = a
    v = -jnp.tanh(log_a) * (1.0 + a * a)
    b_buf[...] = jnp.where(v > 0.0, v * lax.rsqrt(v), 0.0) * (ig * xc_buf[...])

    local = _pos(b_buf, 0)
    decay = _pos(a_buf, 0)
    for j in range(1, CHUNK):
        rows = slice(j * SUBLANES, (j + 1) * SUBLANES)
        a_j = a_buf[rows, :]
        local = a_j * local + b_buf[rows, :]
        decay = a_j * decay
        b_buf[rows, :] = local
        a_buf[rows, :] = decay
    chunk_end = _scan_chunks(decay, local, state)
    chunk_entry = _from_previous_chunk(chunk_end, state)
    h = (b_buf[...].reshape(CHUNK, SUBLANES, D_RNN)
         + a_buf[...].reshape(CHUNK, SUBLANES, D_RNN) * chunk_entry[None]).reshape(MIX_TILE, D_RNN)

    y = (h * gate_buf[...]).astype(BF16)
    return x + _mm(y, wout_ref[...]), jnp.broadcast_to(chunk_end[SUBLANES - 1:, :], state.shape)


def _lru_kernel(x_ref, g_ref, win_ref, cw_ref, cb_ref, wa_ref, ba_ref, wx_ref, bx_ref, ap_ref, wout_ref, o_ref,
                xb_ext, a_buf, b_buf, xc_buf, gate_buf, tail_ref, state_ref):
    @pl.when(pl.program_id(1) == 0)
    def _():
        tail_ref[...] = jnp.zeros_like(tail_ref)
        state_ref[...] = jnp.zeros_like(state_ref)

    blocks = x_ref.shape[0] // MIX_TILE
    xb_tail = tail_ref[...]
    for r in range(blocks):
        rows = slice(r * MIX_TILE, (r + 1) * MIX_TILE)
        xb_tail = _lru_project(x_ref[rows, :], xb_tail, g_ref, win_ref, cw_ref, cb_ref, wa_ref, wx_ref,
                               xb_ext.at[r], a_buf.at[r], b_buf.at[r], xc_buf.at[r], gate_buf.at[r])
    tail_ref[...] = xb_tail
    state = state_ref[...]
    for r in range(blocks):
        rows = slice(r * MIX_TILE, (r + 1) * MIX_TILE)
        o_ref[rows, :], state = _lru_recur(x_ref[rows, :], state, ba_ref, bx_ref, ap_ref, wout_ref,
                                           a_buf.at[r], b_buf.at[r], xc_buf.at[r], gate_buf.at[r])
    state_ref[...] = state


def _window_sum(ext, win):
    cur = ext[(POOL_BOUNDARY - (win - 1)) * SUBLANES:, :]
    step = 1
    while step < win:
        cur = cur[step * SUBLANES:, :] + cur[:-step * SUBLANES, :]
        step *= 2
    return cur


def _pool_kernel(x_ref, g_ref, w_ref, b_ref, sc_ref, o_ref, hn_ext, tail_ref):
    seq_tile = pl.program_id(1)

    @pl.when(seq_tile == 0)
    def _():
        tail_ref[...] = jnp.zeros_like(tail_ref)

    hn_tail = tail_ref[...]
    row = lax.broadcasted_iota(jnp.int32, (MIX_TILE, POOL_GROUP_DIM), 0)
    t_in_block = (row & (SUBLANES - 1)) * CHUNK + lax.shift_right_logical(row, SUBLANES.bit_length() - 1)
    for r in range(x_ref.shape[0] // MIX_TILE):
        rows = slice(r * MIX_TILE, (r + 1) * MIX_TILE)
        x = x_ref[rows, :]
        hn = _rms(x, g_ref[...])
        ext = hn_ext.at[r]
        hn_tail = _fill_boundary(ext, hn_tail, hn, POOL_BOUNDARY)
        t = seq_tile * x_ref.shape[0] + r * MIX_TILE + t_in_block
        for gi, win in enumerate(POOL_WINDOWS):
            cols = slice(gi * POOL_GROUP_DIM, (gi + 1) * POOL_GROUP_DIM)
            count = jnp.minimum(t + 1, win).astype(F32)
            u = (_window_sum(ext[:, cols], win) / count - hn[:, cols]).astype(BF16)
            y = _mm(u, w_ref[gi])
            o_ref[rows, cols] = x[:, cols] + (y + b_ref[:, cols]) * sc_ref[:, cols]
    tail_ref[...] = hn_tail


def _layer_spec(stacked, layer):
    zeros = (0,) * (stacked.ndim - 1)
    return pl.BlockSpec((None,) + stacked.shape[1:], lambda b, s: (layer,) + zeros, pipeline_mode=pl.Buffered(1))


def _tiled_call(kernel, name, n_batch, seq, tiled_inputs, layer_inputs, scratch_shapes=(), convert=()):
    tiles = seq // GRID_TILE
    steps = n_batch * tiles

    def row_spec(width, copy=0):
        base = copy * n_batch * tiles
        return pl.BlockSpec((GRID_TILE, width), lambda b, s: (base + b * tiles + s, 0))

    def slab_in(stack, layer):
        _, rows, cols = stack.shape
        return pl.BlockSpec((None, rows // steps, cols), lambda b, s: (layer, b * tiles + s, 0))

    def slab_out(stack):
        _, rows, cols = stack.shape
        return pl.BlockSpec((rows // steps, cols), lambda b, s: (b * tiles + s, 0))

    for stack, _ in convert:
        assert stack.shape[1] % (steps * BF16_SUBLANES) == 0, stack.shape
    n_in = len(tiled_inputs) + len(layer_inputs)
    n_cv = len(convert)

    def body(*refs):
        ins, cv_in = refs[:n_in], refs[n_in:n_in + n_cv]
        o_ref, cv_out = refs[n_in + n_cv], refs[n_in + n_cv + 1:n_in + 2 * n_cv + 1]
        for src, dst in zip(cv_in, cv_out):
            dst[...] = src[...].astype(BF16)
        kernel(*ins, o_ref, *refs[n_in + 2 * n_cv + 1:])

    return pl.pallas_call(
        body,
        name=name,
        grid=(n_batch, tiles),
        in_specs=[row_spec(a.shape[1], k) for a, k in tiled_inputs] + [_layer_spec(a, l) for a, l in layer_inputs]
        + [slab_in(a, l) for a, l in convert],
        out_specs=[row_spec(D_MODEL)] + [slab_out(a) for a, _ in convert],
        out_shape=[jax.ShapeDtypeStruct((n_batch * seq, D_MODEL), F32)]
        + [jax.ShapeDtypeStruct(a.shape[1:], BF16) for a, _ in convert],
        scratch_shapes=list(scratch_shapes),
        compiler_params=pltpu.CompilerParams(
            dimension_semantics=("arbitrary", "arbitrary"), vmem_limit_bytes=VMEM_LIMIT_BYTES),
    )(*[a for a, _ in tiled_inputs], *[a for a, _ in layer_inputs], *[a for a, _ in convert])


def _rows_of(stacked):
    return stacked.reshape(stacked.shape[0], 1, stacked.shape[1])


def _block_diag(w):
    layers, heads, d, _ = w.shape
    n = heads * d
    col = jnp.arange(n)
    spread = (col[None, :] % d == jnp.arange(d)[:, None]).astype(BF16)
    tiled = jnp.einsum("lrj,jc->lrc", w.reshape(layers, n, d).astype(BF16), spread, preferred_element_type=F32)
    same_head = (col[:, None] // d) == (col[None, :] // d)
    return jnp.where(same_head[None], tiled, 0.0).astype(BF16)


def kernel(x, p, ffn1_norm, ffn1_w_gate, ffn1_w_up, ffn1_w_down, mix_norm, lru_w_in, lru_conv_w, lru_conv_b, lru_w_a, lru_b_a, lru_w_x, lru_b_x, lru_a_param, lru_w_out, pool_w, pool_b, pool_scale, ffn2_norm, ffn2_w_gate, ffn2_w_up, ffn2_w_down, ple_norm, ple_w_gate, ple_w_proj, final_norm):
    n_batch, seq, d_model = x.shape
    depth = p.shape[0]
    assert d_model == D_MODEL and seq % GRID_TILE == 0 and GRID_TILE % MIX_TILE == 0
    call = functools.partial(_tiled_call, n_batch=n_batch, seq=seq)
    one = lambda w: (w.reshape((1,) + w.shape), 0)
    ffn1_f32 = [ffn1_w_gate, ffn1_w_up, ffn1_w_down]
    ffn2_f32 = [ffn2_w_gate, ffn2_w_up, ffn2_w_down, ple_w_gate, ple_w_proj]
    pool_w_rows = pool_w.reshape(pool_w.shape[0], -1, POOL_GROUP_DIM)
    g1, g2, gm, gp = _rows_of(ffn1_norm), _rows_of(ffn2_norm), _rows_of(mix_norm), _rows_of(ple_norm)
    wa, wx = _block_diag(lru_w_a), _block_diag(lru_w_x)
    lru_rows = [_rows_of(v) for v in (lru_conv_b, lru_b_a, lru_b_x, lru_a_param)]
    pool_rows = [_rows_of(pool_b), _rows_of(pool_scale)]
    final_g = final_norm.reshape(1, 1, D_MODEL)
    p_rows = p.reshape(depth * n_batch * seq, PLE_DIM)
    blocks = GRID_TILE // MIX_TILE

    h = x.reshape(n_batch * seq, D_MODEL)
    ffn1_bf = [w[0].astype(BF16) for w in ffn1_f32]
    for i in range(depth):
        j = i // 2
        lru_layer = i % 2 == 0
        mixer_f32 = [lru_w_in, lru_w_out] if lru_layer else [pool_w_rows]
        h, *cast = call(functools.partial(_ffn_kernel, interleave_input=(i == 0)), "ffn1", tiled_inputs=[(h, 0)],
                        layer_inputs=[(g1, i)] + [one(w) for w in ffn1_bf],
                        convert=[(w, i) for w in ffn2_f32] + [(w, j) for w in mixer_f32])
        ffn2_bf, mixer_bf = cast[:len(ffn2_f32)], cast[len(ffn2_f32):]
        if lru_layer:
            w_in, w_out = mixer_bf
            cb, ba, bx, ap = lru_rows
            h, = call(_lru_kernel, "lru", tiled_inputs=[(h, 0)],
                      layer_inputs=[(gm, i), one(w_in), (lru_conv_w, j), (cb, j), (wa, j), (ba, j), (wx, j), (bx, j),
                                    (ap, j), one(w_out)],
                      scratch_shapes=[pltpu.VMEM((blocks, CONV_BOUNDARY * SUBLANES + MIX_TILE, D_RNN), F32),
                                      pltpu.VMEM((blocks, MIX_TILE, D_RNN), F32),
                                      pltpu.VMEM((blocks, MIX_TILE, D_RNN), F32),
                                      pltpu.VMEM((blocks, MIX_TILE, D_RNN), F32),
                                      pltpu.VMEM((blocks, MIX_TILE, D_RNN), F32),
                                      pltpu.VMEM((CONV_BOUNDARY * SUBLANES, D_RNN), F32),
                                      pltpu.VMEM((SUBLANES, D_RNN), F32)])
        else:
            w_pool = mixer_bf[0].reshape(pool_w.shape[1:])
            h, = call(_pool_kernel, "pool", tiled_inputs=[(h, 0)],
                      layer_inputs=[(gm, i), one(w_pool)] + [(v, j) for v in pool_rows],
                      scratch_shapes=[pltpu.VMEM((blocks, POOL_BOUNDARY * SUBLANES + MIX_TILE, D_MODEL), F32),
                                      pltpu.VMEM((POOL_BOUNDARY * SUBLANES, D_MODEL), F32)])
        last = i == depth - 1
        wg, wu, wd, wpg, wpp = ffn2_bf
        h, *ffn1_bf = call(functools.partial(_ffn_ple_kernel, final_norm=last), "ffn2_ple",
                           tiled_inputs=[(h, 0), (p_rows, i)],
                           layer_inputs=[(g2, i), one(wg), one(wu), one(wd), (gp, i), one(wpg), one(wpp),
                                         (final_g, 0)],
                           convert=[] if last else [(w, i + 1) for w in ffn1_f32])
    return h.reshape(n_batch, seq, D_MODEL)
```

```python
import functools

import jax
import jax.numpy as jnp
from jax import lax
from jax.experimental import pallas as pl
from jax.experimental.pallas import tpu as pltpu

D_MODEL = 1024
D_FF = 2816
D_RNN = 1280
LRU_HEADS = 16
LRU_HEAD_DIM = 80
CONV_WIDTH = 4
LRU_C = 8.0
POOL_WINDOWS = (2, 4, 8, 16)
POOL_GROUP_DIM = 256
PLE_DIM = 256
RMS_EPS = 1e-6

SUBLANES = 8
LANES = 128
BF16_SUBLANES = 16
GATE_CHUNK = 256
GATE_WINDOW = 512
FF_CHUNK = 256
MIX_TILE = 512
CHUNK = MIX_TILE // SUBLANES
GRID_TILE = 1024
CONV_BOUNDARY = CONV_WIDTH - 1
POOL_BOUNDARY = max(POOL_WINDOWS) - 1
VMEM_LIMIT_BYTES = 56 * 1024 * 1024

F32 = jnp.float32
BF16 = jnp.bfloat16


def _rms(x, g):
    return x * lax.rsqrt(jnp.mean(x * x, axis=-1, keepdims=True) + RMS_EPS) * g


def _mm(a, w):
    return jnp.dot(a, w, preferred_element_type=F32)


def _interleave(x):
    w = x.shape[1]
    return jnp.swapaxes(x.reshape(SUBLANES, CHUNK, w), 0, 1).reshape(MIX_TILE, w)


def _deinterleave(x):
    w = x.shape[1]
    return jnp.swapaxes(x.reshape(CHUNK, SUBLANES, w), 0, 1).reshape(MIX_TILE, w)


def _pos(x, j):
    return x[j * SUBLANES:(j + 1) * SUBLANES, :]


def _from_previous_chunk(cur, prev_tile):
    chunk = lax.broadcasted_iota(jnp.int32, cur.shape, 0)
    return pltpu.roll(jnp.where(chunk == SUBLANES - 1, prev_tile, cur), 1, axis=0)


def _fill_boundary(ext_ref, prev_tail, cur, n):
    for i in range(n):
        ext_ref[i * SUBLANES:(i + 1) * SUBLANES, :] = _from_previous_chunk(_pos(cur, CHUNK - n + i), _pos(prev_tail, i))
    ext_ref[n * SUBLANES:, :] = cur
    return cur[(CHUNK - n) * SUBLANES:, :]


def _swiglu_residual(x, g_ref, wg_ref, wu_ref, wd_ref):
    xn = _rms(x, g_ref[...]).astype(BF16)
    y = None
    for c in range(D_FF // FF_CHUNK):
        cols = slice(c * FF_CHUNK, (c + 1) * FF_CHUNK)
        gate = _mm(xn, wg_ref[:, cols])
        up = _mm(xn, wu_ref[:, cols])
        act = (gate * jax.nn.sigmoid(gate) * up).astype(BF16)
        part = _mm(act, wd_ref[cols, :])
        y = part if y is None else y + part
    return x + 0.5 * y


def _ffn_kernel(x_ref, g_ref, wg_ref, wu_ref, wd_ref, o_ref, *, interleave_input):
    for r in range(x_ref.shape[0] // MIX_TILE):
        rows = slice(r * MIX_TILE, (r + 1) * MIX_TILE)
        x = x_ref[rows, :]
        if interleave_input:
            x = _interleave(x)
        o_ref[rows, :] = _swiglu_residual(x, g_ref, wg_ref, wu_ref, wd_ref)


def _ffn_ple_kernel(x_ref, p_ref, g_ref, wg_ref, wu_ref, wd_ref, pg_ref, wpg_ref, wpp_ref, fg_ref, o_ref, *,
                    final_norm):
    blocks = [slice(r * MIX_TILE, (r + 1) * MIX_TILE) for r in range(x_ref.shape[0] // MIX_TILE)]
    hs = [_swiglu_residual(x_ref[rows, :], g_ref, wg_ref, wu_ref, wd_ref) for rows in blocks]
    projs = [_mm(_interleave(p_ref[rows, :]).astype(BF16), wpp_ref[...]) for rows in blocks]
    for rows, h, proj in zip(blocks, hs, projs):
        gate = jax.nn.sigmoid(_mm(_rms(h, pg_ref[...]).astype(BF16), wpg_ref[...]))
        h = h + gate * proj
        if final_norm:
            h = _deinterleave(_rms(h, fg_ref[...]))
        o_ref[rows, :] = h


def _gate_window_start(chunk):
    lo = chunk * GATE_CHUNK // LRU_HEAD_DIM * LRU_HEAD_DIM
    hi = -(-(chunk + 1) * GATE_CHUNK // LRU_HEAD_DIM) * LRU_HEAD_DIM
    start = min(lo // LANES * LANES, D_RNN - GATE_WINDOW)
    assert start <= lo and hi <= start + GATE_WINDOW
    return start


def _block_diag_chunk_mm(a, w_ref, chunk):
    k0 = _gate_window_start(chunk)
    cols = slice(chunk * GATE_CHUNK, (chunk + 1) * GATE_CHUNK)
    return _mm(a[:, k0:k0 + GATE_WINDOW], w_ref[k0:k0 + GATE_WINDOW, cols])


def _scan_chunks(decay, inp, state):
    chunk = lax.broadcasted_iota(jnp.int32, decay.shape, 0)
    inp = inp + jnp.where(chunk == 0, decay * state, 0.0)
    step = 1
    while step < SUBLANES:
        seen = chunk >= step
        inp = decay * jnp.where(seen, pltpu.roll(inp, step, axis=0), 0.0) + inp
        decay = decay * jnp.where(seen, pltpu.roll(decay, step, axis=0), 1.0)
        step *= 2
    return inp


def _lru_project(x, xb_tail, g_ref, win_ref, cw_ref, cb_ref, wa_ref, ba_ref, wx_ref, bx_ref, ap_ref,
                 xb_ext, a_buf, b_buf, gate_buf):
    xn = _rms(x, g_ref[...]).astype(BF16)
    xb = _mm(xn, win_ref[:, D_RNN:])
    xb_tail = _fill_boundary(xb_ext, xb_tail, xb, CONV_BOUNDARY)
    xc = cb_ref[...] + cw_ref[CONV_WIDTH - 1:CONV_WIDTH, :] * xb
    for m in range(1, CONV_WIDTH):
        start = (CONV_BOUNDARY - m) * SUBLANES
        xc = xc + cw_ref[CONV_WIDTH - 1 - m:CONV_WIDTH - m, :] * xb_ext[start:start + MIX_TILE, :]
    xcb = xc.astype(BF16)
    log_a_scale = (-LRU_C) * jax.nn.softplus(-ap_ref[...])
    for chunk in range(D_RNN // GATE_CHUNK):
        cols = slice(chunk * GATE_CHUNK, (chunk + 1) * GATE_CHUNK)
        r = jax.nn.sigmoid(_block_diag_chunk_mm(xcb, wa_ref, chunk) + ba_ref[:, cols])
        ig = jax.nn.sigmoid(_block_diag_chunk_mm(xcb, wx_ref, chunk) + bx_ref[:, cols])
        log_a = r * log_a_scale[:, cols]
        a = jnp.exp(log_a)
        v = -jnp.tanh(log_a) * (1.0 + a * a)
        a_buf[:, cols] = a
        b_buf[:, cols] = jnp.where(v > 0.0, v * lax.rsqrt(v), 0.0) * (ig * xc[:, cols])
    gate_buf[...] = jax.nn.gelu(_mm(xn, win_ref[:, :D_RNN]))
    return xb_tail


def _lru_recur(x, state, wout_ref, a_buf, b_buf, gate_buf):
    local = _pos(b_buf, 0)
    decay = _pos(a_buf, 0)
    for j in range(1, CHUNK):
        rows = slice(j * SUBLANES, (j + 1) * SUBLANES)
        a_j = a_buf[rows, :]
        local = a_j * local + b_buf[rows, :]
        decay = a_j * decay
        b_buf[rows, :] = local
        a_buf[rows, :] = decay
    chunk_end = _scan_chunks(decay, local, state)
    chunk_entry = _from_previous_chunk(chunk_end, state)
    h = (b_buf[...].reshape(CHUNK, SUBLANES, D_RNN)
         + a_buf[...].reshape(CHUNK, SUBLANES, D_RNN) * chunk_entry[None]).reshape(MIX_TILE, D_RNN)

    y = (h * gate_buf[...]).astype(BF16)
    return x + _mm(y, wout_ref[...]), jnp.broadcast_to(chunk_end[SUBLANES - 1:, :], state.shape)


def _lru_kernel(x_ref, g_ref, win_ref, cw_ref, cb_ref, wa_ref, ba_ref, wx_ref, bx_ref, ap_ref, wout_ref, o_ref,
                xb_ext, a_buf, b_buf, gate_buf, tail_ref, state_ref):
    @pl.when(pl.program_id(1) == 0)
    def _():
        tail_ref[...] = jnp.zeros_like(tail_ref)
        state_ref[...] = jnp.zeros_like(state_ref)

    blocks = x_ref.shape[0] // MIX_TILE
    xb_tail = tail_ref[...]
    for r in range(blocks):
        rows = slice(r * MIX_TILE, (r + 1) * MIX_TILE)
        xb_tail = _lru_project(x_ref[rows, :], xb_tail, g_ref, win_ref, cw_ref, cb_ref, wa_ref, ba_ref, wx_ref, bx_ref,
                               ap_ref, xb_ext.at[r], a_buf.at[r], b_buf.at[r], gate_buf.at[r])
    tail_ref[...] = xb_tail
    state = state_ref[...]
    for r in range(blocks):
        rows = slice(r * MIX_TILE, (r + 1) * MIX_TILE)
        o_ref[rows, :], state = _lru_recur(x_ref[rows, :], state, wout_ref, a_buf.at[r], b_buf.at[r], gate_buf.at[r])
    state_ref[...] = state


def _window_sum(ext, win):
    cur = ext[(POOL_BOUNDARY - (win - 1)) * SUBLANES:, :]
    step = 1
    while step < win:
        cur = cur[step * SUBLANES:, :] + cur[:-step * SUBLANES, :]
        step *= 2
    return cur


def _pool_kernel(x_ref, g_ref, w_ref, b_ref, sc_ref, o_ref, hn_ext, tail_ref):
    seq_tile = pl.program_id(1)

    @pl.when(seq_tile == 0)
    def _():
        tail_ref[...] = jnp.zeros_like(tail_ref)

    hn_tail = tail_ref[...]
    row = lax.broadcasted_iota(jnp.int32, (MIX_TILE, POOL_GROUP_DIM), 0)
    t_in_block = (row & (SUBLANES - 1)) * CHUNK + lax.shift_right_logical(row, SUBLANES.bit_length() - 1)
    for r in range(x_ref.shape[0] // MIX_TILE):
        rows = slice(r * MIX_TILE, (r + 1) * MIX_TILE)
        x = x_ref[rows, :]
        hn = _rms(x, g_ref[...])
        ext = hn_ext.at[r]
        hn_tail = _fill_boundary(ext, hn_tail, hn, POOL_BOUNDARY)
        t = seq_tile * x_ref.shape[0] + r * MIX_TILE + t_in_block
        for gi, win in enumerate(POOL_WINDOWS):
            cols = slice(gi * POOL_GROUP_DIM, (gi + 1) * POOL_GROUP_DIM)
            count = jnp.minimum(t + 1, win).astype(F32)
            u = (_window_sum(ext[:, cols], win) / count - hn[:, cols]).astype(BF16)
            y = _mm(u, w_ref[gi])
            o_ref[rows, cols] = x[:, cols] + (y + b_ref[:, cols]) * sc_ref[:, cols]
    tail_ref[...] = hn_tail


def _layer_spec(stacked, layer):
    zeros = (0,) * (stacked.ndim - 1)
    return pl.BlockSpec((None,) + stacked.shape[1:], lambda b, s: (layer,) + zeros, pipeline_mode=pl.Buffered(1))


def _tiled_call(kernel, name, n_batch, seq, tiled_inputs, layer_inputs, scratch_shapes=(), convert=()):
    tiles = seq // GRID_TILE
    steps = n_batch * tiles

    def row_spec(width, copy=0):
        base = copy * n_batch * tiles
        return pl.BlockSpec((GRID_TILE, width), lambda b, s: (base + b * tiles + s, 0))

    def slab_in(stack, layer):
        _, rows, cols = stack.shape
        return pl.BlockSpec((None, rows // steps, cols), lambda b, s: (layer, b * tiles + s, 0))

    def slab_out(stack):
        _, rows, cols = stack.shape
        return pl.BlockSpec((rows // steps, cols), lambda b, s: (b * tiles + s, 0))

    for stack, _ in convert:
        assert stack.shape[1] % (steps * BF16_SUBLANES) == 0, stack.shape
    n_in = len(tiled_inputs) + len(layer_inputs)
    n_cv = len(convert)

    def body(*refs):
        ins, cv_in = refs[:n_in], refs[n_in:n_in + n_cv]
        o_ref, cv_out = refs[n_in + n_cv], refs[n_in + n_cv + 1:n_in + 2 * n_cv + 1]
        for src, dst in zip(cv_in, cv_out):
            dst[...] = src[...].astype(BF16)
        kernel(*ins, o_ref, *refs[n_in + 2 * n_cv + 1:])

    return pl.pallas_call(
        body,
        name=name,
        grid=(n_batch, tiles),
        in_specs=[row_spec(a.shape[1], k) for a, k in tiled_inputs] + [_layer_spec(a, l) for a, l in layer_inputs]
        + [slab_in(a, l) for a, l in convert],
        out_specs=[row_spec(D_MODEL)] + [slab_out(a) for a, _ in convert],
        out_shape=[jax.ShapeDtypeStruct((n_batch * seq, D_MODEL), F32)]
        + [jax.ShapeDtypeStruct(a.shape[1:], BF16) for a, _ in convert],
        scratch_shapes=list(scratch_shapes),
        compiler_params=pltpu.CompilerParams(
            dimension_semantics=("arbitrary", "arbitrary"), vmem_limit_bytes=VMEM_LIMIT_BYTES),
    )(*[a for a, _ in tiled_inputs], *[a for a, _ in layer_inputs], *[a for a, _ in convert])


def _rows_of(stacked):
    return stacked.reshape(stacked.shape[0], 1, stacked.shape[1])


def _block_diag(w):
    layers, heads, d, _ = w.shape
    n = heads * d
    col = jnp.arange(n)
    spread = (col[None, :] % d == jnp.arange(d)[:, None]).astype(BF16)
    tiled = jnp.einsum("lrj,jc->lrc", w.reshape(layers, n, d).astype(BF16), spread, preferred_element_type=F32)
    same_head = (col[:, None] // d) == (col[None, :] // d)
    return jnp.where(same_head[None], tiled, 0.0).astype(BF16)


def kernel(x, p, ffn1_norm, ffn1_w_gate, ffn1_w_up, ffn1_w_down, mix_norm, lru_w_in, lru_conv_w, lru_conv_b, lru_w_a, lru_b_a, lru_w_x, lru_b_x, lru_a_param, lru_w_out, pool_w, pool_b, pool_scale, ffn2_norm, ffn2_w_gate, ffn2_w_up, ffn2_w_down, ple_norm, ple_w_gate, ple_w_proj, final_norm):
    n_batch, seq, d_model = x.shape
    depth = p.shape[0]
    assert d_model == D_MODEL and seq % GRID_TILE == 0 and GRID_TILE % MIX_TILE == 0
    call = functools.partial(_tiled_call, n_batch=n_batch, seq=seq)
    one = lambda w: (w.reshape((1,) + w.shape), 0)
    ffn1_f32 = [ffn1_w_gate, ffn1_w_up, ffn1_w_down]
    ffn2_f32 = [ffn2_w_gate, ffn2_w_up, ffn2_w_down, ple_w_gate, ple_w_proj]
    pool_w_rows = pool_w.reshape(pool_w.shape[0], -1, POOL_GROUP_DIM)
    g1, g2, gm, gp = _rows_of(ffn1_norm), _rows_of(ffn2_norm), _rows_of(mix_norm), _rows_of(ple_norm)
    wa, wx = _block_diag(lru_w_a), _block_diag(lru_w_x)
    lru_rows = [_rows_of(v) for v in (lru_conv_b, lru_b_a, lru_b_x, lru_a_param)]
    pool_rows = [_rows_of(pool_b), _rows_of(pool_scale)]
    final_g = final_norm.reshape(1, 1, D_MODEL)
    p_rows = p.reshape(depth * n_batch * seq, PLE_DIM)
    blocks = GRID_TILE // MIX_TILE

    h = x.reshape(n_batch * seq, D_MODEL)
    ffn1_bf = [w[0].astype(BF16) for w in ffn1_f32]
    for i in range(depth):
        j = i // 2
        lru_layer = i % 2 == 0
        mixer_f32 = [lru_w_in, lru_w_out] if lru_layer else [pool_w_rows]
        h, *cast = call(functools.partial(_ffn_kernel, interleave_input=(i == 0)), "ffn1", tiled_inputs=[(h, 0)],
                        layer_inputs=[(g1, i)] + [one(w) for w in ffn1_bf],
                        convert=[(w, i) for w in ffn2_f32] + [(w, j) for w in mixer_f32])
        ffn2_bf, mixer_bf = cast[:len(ffn2_f32)], cast[len(ffn2_f32):]
        if lru_layer:
            w_in, w_out = mixer_bf
            cb, ba, bx, ap = lru_rows
            h, = call(_lru_kernel, "lru", tiled_inputs=[(h, 0)],
                      layer_inputs=[(gm, i), one(w_in), (lru_conv_w, j), (cb, j), (wa, j), (ba, j), (wx, j), (bx, j),
                                    (ap, j), one(w_out)],
                      scratch_shapes=[pltpu.VMEM((blocks, CONV_BOUNDARY * SUBLANES + MIX_TILE, D_RNN), F32),
                                      pltpu.VMEM((blocks, MIX_TILE, D_RNN), F32),
                                      pltpu.VMEM((blocks, MIX_TILE, D_RNN), F32),
                                      pltpu.VMEM((blocks, MIX_TILE, D_RNN), F32),
                                      pltpu.VMEM((CONV_BOUNDARY * SUBLANES, D_RNN), F32),
                                      pltpu.VMEM((SUBLANES, D_RNN), F32)])
        else:
            w_pool = mixer_bf[0].reshape(pool_w.shape[1:])
            h, = call(_pool_kernel, "pool", tiled_inputs=[(h, 0)],
                      layer_inputs=[(gm, i), one(w_pool)] + [(v, j) for v in pool_rows],
                      scratch_shapes=[pltpu.VMEM((blocks, POOL_BOUNDARY * SUBLANES + MIX_TILE, D_MODEL), F32),
                                      pltpu.VMEM((POOL_BOUNDARY * SUBLANES, D_MODEL), F32)])
        last = i == depth - 1
        wg, wu, wd, wpg, wpp = ffn2_bf
        h, *ffn1_bf = call(functools.partial(_ffn_ple_kernel, final_norm=last), "ffn2_ple",
                           tiled_inputs=[(h, 0), (p_rows, i)],
                           layer_inputs=[(g2, i), one(wg), one(wu), one(wd), (gp, i), one(wpg), one(wpp),
                                         (final_g, 0)],
                           convert=[] if last else [(w, i + 1) for w in ffn1_f32])
    return h.reshape(n_batch, seq, D_MODEL)
```

```python
import functools

import jax
import jax.numpy as jnp
from jax import lax
from jax.experimental import pallas as pl
from jax.experimental.pallas import tpu as pltpu

D_MODEL = 1024
D_FF = 2816
D_RNN = 1280
LRU_HEADS = 16
LRU_HEAD_DIM = 80
CONV_WIDTH = 4
LRU_C = 8.0
POOL_WINDOWS = (2, 4, 8, 16)
POOL_GROUP_DIM = 256
PLE_DIM = 256
RMS_EPS = 1e-6

SUBLANES = 8
LANES = 128
BF16_SUBLANES = 16
GATE_CHUNK = 256
GATE_WINDOW = 512
FF_CHUNK = 256
MIX_TILE = 512
CHUNK = MIX_TILE // SUBLANES
GRID_TILE = 1024
CONV_BOUNDARY = CONV_WIDTH - 1
POOL_BOUNDARY = max(POOL_WINDOWS) - 1
VMEM_LIMIT_BYTES = 56 * 1024 * 1024

F32 = jnp.float32
BF16 = jnp.bfloat16


def _rms(x, g):
    return x * lax.rsqrt(jnp.mean(x * x, axis=-1, keepdims=True) + RMS_EPS) * g


def _mm(a, w):
    return jnp.dot(a, w, preferred_element_type=F32)


def _interleave(x):
    w = x.shape[1]
    return jnp.swapaxes(x.reshape(SUBLANES, CHUNK, w), 0, 1).reshape(MIX_TILE, w)


def _deinterleave(x):
    w = x.shape[1]
    return jnp.swapaxes(x.reshape(CHUNK, SUBLANES, w), 0, 1).reshape(MIX_TILE, w)


def _pos(x, j):
    return x[j * SUBLANES:(j + 1) * SUBLANES, :]


def _from_previous_chunk(cur, prev_tile):
    chunk = lax.broadcasted_iota(jnp.int32, cur.shape, 0)
    return pltpu.roll(jnp.where(chunk == SUBLANES - 1, prev_tile, cur), 1, axis=0)


def _fill_boundary(ext_ref, prev_tail, cur, n):
    for i in range(n):
        ext_ref[i * SUBLANES:(i + 1) * SUBLANES, :] = _from_previous_chunk(_pos(cur, CHUNK - n + i), _pos(prev_tail, i))
    ext_ref[n * SUBLANES:, :] = cur
    return cur[(CHUNK - n) * SUBLANES:, :]


def _swiglu_residual(x, g_ref, wg_ref, wu_ref, wd_ref):
    xn = _rms(x, g_ref[...]).astype(BF16)
    y = None
    for c in range(D_FF // FF_CHUNK):
        cols = slice(c * FF_CHUNK, (c + 1) * FF_CHUNK)
        gate = _mm(xn, wg_ref[:, cols])
        up = _mm(xn, wu_ref[:, cols])
        act = (gate * jax.nn.sigmoid(gate) * up).astype(BF16)
        part = _mm(act, wd_ref[cols, :])
        y = part if y is None else y + part
    return x + 0.5 * y


def _ffn_kernel(x_ref, g_ref, wg_ref, wu_ref, wd_ref, o_ref, *, interleave_input):
    for r in range(x_ref.shape[0] // MIX_TILE):
        rows = slice(r * MIX_TILE, (r + 1) * MIX_TILE)
        x = x_ref[rows, :]
        if interleave_input:
            x = _interleave(x)
        o_ref[rows, :] = _swiglu_residual(x, g_ref, wg_ref, wu_ref, wd_ref)


def _ffn_pool_kernel(x_ref, g_ref, wg_ref, wu_ref, wd_ref, mg_ref, pw_ref, pb_ref, ps_ref, o_ref, hn_ext, tail_ref):
    seq_tile = pl.program_id(1)

    @pl.when(seq_tile == 0)
    def _():
        tail_ref[...] = jnp.zeros_like(tail_ref)

    blocks = [slice(r * MIX_TILE, (r + 1) * MIX_TILE) for r in range(x_ref.shape[0] // MIX_TILE)]
    hs = [_swiglu_residual(x_ref[rows, :], g_ref, wg_ref, wu_ref, wd_ref) for rows in blocks]
    hn_tail = tail_ref[...]
    for r, (rows, h) in enumerate(zip(blocks, hs)):
        first_token = seq_tile * x_ref.shape[0] + r * MIX_TILE
        hn_tail = _pool_block(h, hn_tail, first_token, mg_ref, pw_ref, pb_ref, ps_ref, hn_ext.at[r], o_ref, rows)
    tail_ref[...] = hn_tail


def _ffn_ple_kernel(x_ref, p_ref, g_ref, wg_ref, wu_ref, wd_ref, pg_ref, wpg_ref, wpp_ref, fg_ref, o_ref, *,
                    final_norm):
    blocks = [slice(r * MIX_TILE, (r + 1) * MIX_TILE) for r in range(x_ref.shape[0] // MIX_TILE)]
    hs = [_swiglu_residual(x_ref[rows, :], g_ref, wg_ref, wu_ref, wd_ref) for rows in blocks]
    projs = [_mm(_interleave(p_ref[rows, :]).astype(BF16), wpp_ref[...]) for rows in blocks]
    for rows, h, proj in zip(blocks, hs, projs):
        gate = jax.nn.sigmoid(_mm(_rms(h, pg_ref[...]).astype(BF16), wpg_ref[...]))
        h = h + gate * proj
        if final_norm:
            h = _deinterleave(_rms(h, fg_ref[...]))
        o_ref[rows, :] = h


def _gate_window_start(chunk):
    lo = chunk * GATE_CHUNK // LRU_HEAD_DIM * LRU_HEAD_DIM
    hi = -(-(chunk + 1) * GATE_CHUNK // LRU_HEAD_DIM) * LRU_HEAD_DIM
    start = min(lo // LANES * LANES, D_RNN - GATE_WINDOW)
    assert start <= lo and hi <= start + GATE_WINDOW
    return start


def _block_diag_chunk_mm(a, w_ref, chunk):
    k0 = _gate_window_start(chunk)
    cols = slice(chunk * GATE_CHUNK, (chunk + 1) * GATE_CHUNK)
    return _mm(a[:, k0:k0 + GATE_WINDOW], w_ref[k0:k0 + GATE_WINDOW, cols])


def _scan_chunks(decay, inp, state):
    chunk = lax.broadcasted_iota(jnp.int32, decay.shape, 0)
    inp = inp + jnp.where(chunk == 0, decay * state, 0.0)
    step = 1
    while step < SUBLANES:
        seen = chunk >= step
        inp = decay * jnp.where(seen, pltpu.roll(inp, step, axis=0), 0.0) + inp
        decay = decay * jnp.where(seen, pltpu.roll(decay, step, axis=0), 1.0)
        step *= 2
    return inp


def _lru_project(x, xb_tail, g_ref, win_ref, cw_ref, cb_ref, wa_ref, ba_ref, wx_ref, bx_ref, ap_ref,
                 xb_ext, a_buf, b_buf, gate_buf):
    xn = _rms(x, g_ref[...]).astype(BF16)
    xb = _mm(xn, win_ref[:, D_RNN:])
    xb_tail = _fill_boundary(xb_ext, xb_tail, xb, CONV_BOUNDARY)
    xc = cb_ref[...] + cw_ref[CONV_WIDTH - 1:CONV_WIDTH, :] * xb
    for m in range(1, CONV_WIDTH):
        start = (CONV_BOUNDARY - m) * SUBLANES
        xc = xc + cw_ref[CONV_WIDTH - 1 - m:CONV_WIDTH - m, :] * xb_ext[start:start + MIX_TILE, :]
    xcb = xc.astype(BF16)
    log_a_scale = (-LRU_C) * jax.nn.softplus(-ap_ref[...])
    for chunk in range(D_RNN // GATE_CHUNK):
        cols = slice(chunk * GATE_CHUNK, (chunk + 1) * GATE_CHUNK)
        r = jax.nn.sigmoid(_block_diag_chunk_mm(xcb, wa_ref, chunk) + ba_ref[:, cols])
        ig = jax.nn.sigmoid(_block_diag_chunk_mm(xcb, wx_ref, chunk) + bx_ref[:, cols])
        log_a = r * log_a_scale[:, cols]
        a = jnp.exp(log_a)
        v = -jnp.tanh(log_a) * (1.0 + a * a)
        a_buf[:, cols] = a
        b_buf[:, cols] = jnp.where(v > 0.0, v * lax.rsqrt(v), 0.0) * (ig * xc[:, cols])
    gate_buf[...] = jax.nn.gelu(_mm(xn, win_ref[:, :D_RNN]))
    return xb_tail


def _lru_recur(x, state, wout_ref, a_buf, b_buf, gate_buf):
    local = _pos(b_buf, 0)
    decay = _pos(a_buf, 0)
    for j in range(1, CHUNK):
        rows = slice(j * SUBLANES, (j + 1) * SUBLANES)
        a_j = a_buf[rows, :]
        local = a_j * local + b_buf[rows, :]
        decay = a_j * decay
        b_buf[rows, :] = local
        a_buf[rows, :] = decay
    chunk_end = _scan_chunks(decay, local, state)
    chunk_entry = _from_previous_chunk(chunk_end, state)
    h = (b_buf[...].reshape(CHUNK, SUBLANES, D_RNN)
         + a_buf[...].reshape(CHUNK, SUBLANES, D_RNN) * chunk_entry[None]).reshape(MIX_TILE, D_RNN)

    y = (h * gate_buf[...]).astype(BF16)
    return x + _mm(y, wout_ref[...]), jnp.broadcast_to(chunk_end[SUBLANES - 1:, :], state.shape)


def _lru_kernel(x_ref, g_ref, win_ref, cw_ref, cb_ref, wa_ref, ba_ref, wx_ref, bx_ref, ap_ref, wout_ref, o_ref,
                xb_ext, a_buf, b_buf, gate_buf, tail_ref, state_ref):
    @pl.when(pl.program_id(1) == 0)
    def _():
        tail_ref[...] = jnp.zeros_like(tail_ref)
        state_ref[...] = jnp.zeros_like(state_ref)

    blocks = x_ref.shape[0] // MIX_TILE
    xb_tail = tail_ref[...]
    for r in range(blocks):
        rows = slice(r * MIX_TILE, (r + 1) * MIX_TILE)
        xb_tail = _lru_project(x_ref[rows, :], xb_tail, g_ref, win_ref, cw_ref, cb_ref, wa_ref, ba_ref, wx_ref, bx_ref,
                               ap_ref, xb_ext.at[r], a_buf.at[r], b_buf.at[r], gate_buf.at[r])
    tail_ref[...] = xb_tail
    state = state_ref[...]
    for r in range(blocks):
        rows = slice(r * MIX_TILE, (r + 1) * MIX_TILE)
        o_ref[rows, :], state = _lru_recur(x_ref[rows, :], state, wout_ref, a_buf.at[r], b_buf.at[r], gate_buf.at[r])
    state_ref[...] = state


def _window_sum(ext, win):
    cur = ext[(POOL_BOUNDARY - (win - 1)) * SUBLANES:, :]
    step = 1
    while step < win:
        cur = cur[step * SUBLANES:, :] + cur[:-step * SUBLANES, :]
        step *= 2
    return cur


def _pool_block(x, hn_tail, first_token, g_ref, w_ref, b_ref, sc_ref, ext, o_ref, rows):
    hn = _rms(x, g_ref[...])
    hn_tail = _fill_boundary(ext, hn_tail, hn, POOL_BOUNDARY)
    row = lax.broadcasted_iota(jnp.int32, (MIX_TILE, POOL_GROUP_DIM), 0)
    t = first_token + (row & (SUBLANES - 1)) * CHUNK + lax.shift_right_logical(row, SUBLANES.bit_length() - 1)
    for gi, win in enumerate(POOL_WINDOWS):
        cols = slice(gi * POOL_GROUP_DIM, (gi + 1) * POOL_GROUP_DIM)
        count = jnp.minimum(t + 1, win).astype(F32)
        u = (_window_sum(ext[:, cols], win) / count - hn[:, cols]).astype(BF16)
        y = _mm(u, w_ref[gi])
        o_ref[rows, cols] = x[:, cols] + (y + b_ref[:, cols]) * sc_ref[:, cols]
    return hn_tail


def _layer_spec(stacked, layer):
    zeros = (0,) * (stacked.ndim - 1)
    return pl.BlockSpec((None,) + stacked.shape[1:], lambda b, s: (layer,) + zeros, pipeline_mode=pl.Buffered(1))


def _tiled_call(kernel, name, n_batch, seq, tiled_inputs, layer_inputs, scratch_shapes=(), convert=()):
    tiles = seq // GRID_TILE
    steps = n_batch * tiles

    def row_spec(width, copy=0):
        base = copy * n_batch * tiles
        return pl.BlockSpec((GRID_TILE, width), lambda b, s: (base + b * tiles + s, 0))

    def slab_in(stack, layer):
        _, rows, cols = stack.shape
        return pl.BlockSpec((None, rows // steps, cols), lambda b, s: (layer, b * tiles + s, 0))

    def slab_out(stack):
        _, rows, cols = stack.shape
        return pl.BlockSpec((rows // steps, cols), lambda b, s: (b * tiles + s, 0))

    for stack, _ in convert:
        assert stack.shape[1] % (steps * BF16_SUBLANES) == 0, stack.shape
    n_in = len(tiled_inputs) + len(layer_inputs)
    n_cv = len(convert)

    def body(*refs):
        ins, cv_in = refs[:n_in], refs[n_in:n_in + n_cv]
        o_ref, cv_out = refs[n_in + n_cv], refs[n_in + n_cv + 1:n_in + 2 * n_cv + 1]
        for src, dst in zip(cv_in, cv_out):
            dst[...] = src[...].astype(BF16)
        kernel(*ins, o_ref, *refs[n_in + 2 * n_cv + 1:])

    return pl.pallas_call(
        body,
        name=name,
        grid=(n_batch, tiles),
        in_specs=[row_spec(a.shape[1], k) for a, k in tiled_inputs] + [_layer_spec(a, l) for a, l in layer_inputs]
        + [slab_in(a, l) for a, l in convert],
        out_specs=[row_spec(D_MODEL)] + [slab_out(a) for a, _ in convert],
        out_shape=[jax.ShapeDtypeStruct((n_batch * seq, D_MODEL), F32)]
        + [jax.ShapeDtypeStruct(a.shape[1:], BF16) for a, _ in convert],
        scratch_shapes=list(scratch_shapes),
        compiler_params=pltpu.CompilerParams(
            dimension_semantics=("arbitrary", "arbitrary"), vmem_limit_bytes=VMEM_LIMIT_BYTES),
    )(*[a for a, _ in tiled_inputs], *[a for a, _ in layer_inputs], *[a for a, _ in convert])


def _rows_of(stacked):
    return stacked.reshape(stacked.shape[0], 1, stacked.shape[1])


def _block_diag(w):
    layers, heads, d, _ = w.shape
    n = heads * d
    col = jnp.arange(n)
    spread = (col[None, :] % d == jnp.arange(d)[:, None]).astype(BF16)
    tiled = jnp.einsum("lrj,jc->lrc", w.reshape(layers, n, d).astype(BF16), spread, preferred_element_type=F32)
    same_head = (col[:, None] // d) == (col[None, :] // d)
    return jnp.where(same_head[None], tiled, 0.0).astype(BF16)


def kernel(x, p, ffn1_norm, ffn1_w_gate, ffn1_w_up, ffn1_w_down, mix_norm, lru_w_in, lru_conv_w, lru_conv_b, lru_w_a, lru_b_a, lru_w_x, lru_b_x, lru_a_param, lru_w_out, pool_w, pool_b, pool_scale, ffn2_norm, ffn2_w_gate, ffn2_w_up, ffn2_w_down, ple_norm, ple_w_gate, ple_w_proj, final_norm):
    n_batch, seq, d_model = x.shape
    depth = p.shape[0]
    assert d_model == D_MODEL and seq % GRID_TILE == 0 and GRID_TILE % MIX_TILE == 0
    call = functools.partial(_tiled_call, n_batch=n_batch, seq=seq)
    one = lambda w: (w.reshape((1,) + w.shape), 0)
    ffn1_f32 = [ffn1_w_gate, ffn1_w_up, ffn1_w_down]
    ffn2_f32 = [ffn2_w_gate, ffn2_w_up, ffn2_w_down, ple_w_gate, ple_w_proj]
    pool_w_rows = pool_w.reshape(pool_w.shape[0], -1, POOL_GROUP_DIM)
    g1, g2, gm, gp = _rows_of(ffn1_norm), _rows_of(ffn2_norm), _rows_of(mix_norm), _rows_of(ple_norm)
    wa, wx = _block_diag(lru_w_a), _block_diag(lru_w_x)
    lru_rows = [_rows_of(v) for v in (lru_conv_b, lru_b_a, lru_b_x, lru_a_param)]
    pool_rows = [_rows_of(pool_b), _rows_of(pool_scale)]
    final_g = final_norm.reshape(1, 1, D_MODEL)
    p_rows = p.reshape(depth * n_batch * seq, PLE_DIM)
    blocks = GRID_TILE // MIX_TILE

    h = x.reshape(n_batch * seq, D_MODEL)
    ffn1_bf = [w[0].astype(BF16) for w in ffn1_f32]
    pool_bf = None
    for i in range(depth):
        j = i // 2
        ffn1_layers = [(g1, i)] + [one(w) for w in ffn1_bf]
        if i % 2 == 0:
            h, *cast = call(functools.partial(_ffn_kernel, interleave_input=(i == 0)), "ffn1", tiled_inputs=[(h, 0)],
                            layer_inputs=ffn1_layers,
                            convert=[(w, i) for w in ffn2_f32] + [(lru_w_in, j), (lru_w_out, j)])
            *ffn2_bf, w_in, w_out = cast
            cb, ba, bx, ap = lru_rows
            h, = call(_lru_kernel, "lru", tiled_inputs=[(h, 0)],
                      layer_inputs=[(gm, i), one(w_in), (lru_conv_w, j), (cb, j), (wa, j), (ba, j), (wx, j), (bx, j),
                                    (ap, j), one(w_out)],
                      scratch_shapes=[pltpu.VMEM((blocks, CONV_BOUNDARY * SUBLANES + MIX_TILE, D_RNN), F32),
                                      pltpu.VMEM((blocks, MIX_TILE, D_RNN), F32),
                                      pltpu.VMEM((blocks, MIX_TILE, D_RNN), F32),
                                      pltpu.VMEM((blocks, MIX_TILE, D_RNN), F32),
                                      pltpu.VMEM((CONV_BOUNDARY * SUBLANES, D_RNN), F32),
                                      pltpu.VMEM((SUBLANES, D_RNN), F32)])
        else:
            h, *ffn2_bf = call(_ffn_pool_kernel, "ffn1_pool", tiled_inputs=[(h, 0)],
                               layer_inputs=ffn1_layers + [(gm, i), one(pool_bf.reshape(pool_w.shape[1:]))]
                               + [(v, j) for v in pool_rows],
                               scratch_shapes=[pltpu.VMEM((blocks, POOL_BOUNDARY * SUBLANES + MIX_TILE, D_MODEL), F32),
                                               pltpu.VMEM((POOL_BOUNDARY * SUBLANES, D_MODEL), F32)],
                               convert=[(w, i) for w in ffn2_f32])
        last = i == depth - 1
        wg, wu, wd, wpg, wpp = ffn2_bf
        next_pool = [(pool_w_rows, (i + 1) // 2)] if (i + 1) % 2 == 1 else []
        h, *cast = call(functools.partial(_ffn_ple_kernel, final_norm=last), "ffn2_ple",
                        tiled_inputs=[(h, 0), (p_rows, i)],
                        layer_inputs=[(g2, i), one(wg), one(wu), one(wd), (gp, i), one(wpg), one(wpp), (final_g, 0)],
                        convert=[] if last else [(w, i + 1) for w in ffn1_f32] + next_pool)
        if not last:
            ffn1_bf, pool_bf = cast[:len(ffn1_f32)], (cast[len(ffn1_f32)] if next_pool else None)
    return h.reshape(n_batch, seq, D_MODEL)
```

```python
import functools

import jax
import jax.numpy as jnp
from jax import lax
from jax.experimental import pallas as pl
from jax.experimental.pallas import tpu as pltpu

D_MODEL = 1024
D_FF = 2816
D_RNN = 1280
LRU_HEADS = 16
LRU_HEAD_DIM = 80
CONV_WIDTH = 4
LRU_C = 8.0
POOL_WINDOWS = (2, 4, 8, 16)
POOL_GROUP_DIM = 256
PLE_DIM = 256
RMS_EPS = 1e-6

SUBLANES = 8
LANES = 128
BF16_SUBLANES = 16
GATE_CHUNK = 256
GATE_WINDOW = 512
FF_CHUNK = 256
MIX_TILE = 512
CHUNK = MIX_TILE // SUBLANES
GRID_TILE = 1024
CONV_BOUNDARY = CONV_WIDTH - 1
POOL_BOUNDARY = max(POOL_WINDOWS) - 1
VMEM_LIMIT_BYTES = 60000 * 1024

F32 = jnp.float32
BF16 = jnp.bfloat16


def _rms(x, g):
    return x * lax.rsqrt(jnp.mean(x * x, axis=-1, keepdims=True) + RMS_EPS) * g


def _mm(a, w):
    return jnp.dot(a, w, preferred_element_type=F32)


def _interleave(x):
    w = x.shape[1]
    return jnp.swapaxes(x.reshape(SUBLANES, CHUNK, w), 0, 1).reshape(MIX_TILE, w)


def _deinterleave(x):
    w = x.shape[1]
    return jnp.swapaxes(x.reshape(CHUNK, SUBLANES, w), 0, 1).reshape(MIX_TILE, w)


def _pos(x, j):
    return x[j * SUBLANES:(j + 1) * SUBLANES, :]


def _from_previous_chunk(cur, prev_tile):
    chunk = lax.broadcasted_iota(jnp.int32, cur.shape, 0)
    return pltpu.roll(jnp.where(chunk == SUBLANES - 1, prev_tile, cur), 1, axis=0)


def _fill_boundary(ext_ref, prev_tail, cur, n):
    for i in range(n):
        ext_ref[i * SUBLANES:(i + 1) * SUBLANES, :] = _from_previous_chunk(_pos(cur, CHUNK - n + i), _pos(prev_tail, i))
    ext_ref[n * SUBLANES:, :] = cur
    return cur[(CHUNK - n) * SUBLANES:, :]


def _swiglu_residual(x, g_ref, wg_ref, wu_ref, wd_ref, wait_chunk=None):
    xn = _rms(x, g_ref[...]).astype(BF16)
    y = None
    for c in range(D_FF // FF_CHUNK):
        cols = slice(c * FF_CHUNK, (c + 1) * FF_CHUNK)
        if wait_chunk is not None:
            wait_chunk(c)
        gate = _mm(xn, wg_ref[:, cols])
        up = _mm(xn, wu_ref[:, cols])
        act = (gate * jax.nn.sigmoid(gate) * up).astype(BF16)
        part = _mm(act, wd_ref[cols, :])
        y = part if y is None else y + part
    return x + 0.5 * y


def _ffn_kernel(x_ref, g_ref, wg_ref, wu_ref, wd_ref, o_ref, *, interleave_input, wait_chunk):
    for r in range(x_ref.shape[0] // MIX_TILE):
        rows = slice(r * MIX_TILE, (r + 1) * MIX_TILE)
        x = x_ref[rows, :]
        if interleave_input:
            x = _interleave(x)
        o_ref[rows, :] = _swiglu_residual(x, g_ref, wg_ref, wu_ref, wd_ref, wait_chunk if r == 0 else None)


def _ffn_pool_kernel(x_ref, g_ref, wg_ref, wu_ref, wd_ref, mg_ref, pw_ref, pb_ref, ps_ref, o_ref, hn_ext, tail_ref, *,
                     wait_chunk):
    seq_tile = pl.program_id(1)

    @pl.when(seq_tile == 0)
    def _():
        tail_ref[...] = jnp.zeros_like(tail_ref)

    blocks = [slice(r * MIX_TILE, (r + 1) * MIX_TILE) for r in range(x_ref.shape[0] // MIX_TILE)]
    hs = [_swiglu_residual(x_ref[rows, :], g_ref, wg_ref, wu_ref, wd_ref, wait_chunk if r == 0 else None)
          for r, rows in enumerate(blocks)]
    hn_tail = tail_ref[...]
    for r, (rows, h) in enumerate(zip(blocks, hs)):
        first_token = seq_tile * x_ref.shape[0] + r * MIX_TILE
        hn_tail = _pool_block(h, hn_tail, first_token, mg_ref, pw_ref, pb_ref, ps_ref, hn_ext.at[r], o_ref, rows)
    tail_ref[...] = hn_tail


def _ffn_ple_kernel(x_ref, p_ref, g_ref, wg_ref, wu_ref, wd_ref, pg_ref, wpg_ref, wpp_ref, fg_ref, o_ref, *,
                    final_norm, wait_chunk):
    blocks = [slice(r * MIX_TILE, (r + 1) * MIX_TILE) for r in range(x_ref.shape[0] // MIX_TILE)]
    hs = [_swiglu_residual(x_ref[rows, :], g_ref, wg_ref, wu_ref, wd_ref, wait_chunk if r == 0 else None)
          for r, rows in enumerate(blocks)]
    projs = [_mm(_interleave(p_ref[rows, :]).astype(BF16), wpp_ref[...]) for rows in blocks]
    for rows, h, proj in zip(blocks, hs, projs):
        gate = jax.nn.sigmoid(_mm(_rms(h, pg_ref[...]).astype(BF16), wpg_ref[...]))
        h = h + gate * proj
        if final_norm:
            h = _deinterleave(_rms(h, fg_ref[...]))
        o_ref[rows, :] = h


def _gate_window_start(chunk):
    lo = chunk * GATE_CHUNK // LRU_HEAD_DIM * LRU_HEAD_DIM
    hi = -(-(chunk + 1) * GATE_CHUNK // LRU_HEAD_DIM) * LRU_HEAD_DIM
    start = min(lo // LANES * LANES, D_RNN - GATE_WINDOW)
    assert start <= lo and hi <= start + GATE_WINDOW
    return start


def _block_diag_chunk_mm(a, w_ref, chunk):
    k0 = _gate_window_start(chunk)
    cols = slice(chunk * GATE_CHUNK, (chunk + 1) * GATE_CHUNK)
    return _mm(a[:, k0:k0 + GATE_WINDOW], w_ref[k0:k0 + GATE_WINDOW, cols])


def _scan_chunks(decay, inp, state):
    chunk = lax.broadcasted_iota(jnp.int32, decay.shape, 0)
    inp = inp + jnp.where(chunk == 0, decay * state, 0.0)
    step = 1
    while step < SUBLANES:
        seen = chunk >= step
        inp = decay * jnp.where(seen, pltpu.roll(inp, step, axis=0), 0.0) + inp
        decay = decay * jnp.where(seen, pltpu.roll(decay, step, axis=0), 1.0)
        step *= 2
    return inp


def _lru_project(x, xb_tail, g_ref, win_ref, cw_ref, cb_ref, wa_ref, ba_ref, wx_ref, bx_ref, ap_ref,
                 xb_ext, a_buf, b_buf, gate_buf):
    xn = _rms(x, g_ref[...]).astype(BF16)
    xb = _mm(xn, win_ref[:, D_RNN:])
    xb_tail = _fill_boundary(xb_ext, xb_tail, xb, CONV_BOUNDARY)
    xc = cb_ref[...] + cw_ref[CONV_WIDTH - 1:CONV_WIDTH, :] * xb
    for m in range(1, CONV_WIDTH):
        start = (CONV_BOUNDARY - m) * SUBLANES
        xc = xc + cw_ref[CONV_WIDTH - 1 - m:CONV_WIDTH - m, :] * xb_ext[start:start + MIX_TILE, :]
    xcb = xc.astype(BF16)
    log_a_scale = (-LRU_C) * jax.nn.softplus(-ap_ref[...])
    for chunk in range(D_RNN // GATE_CHUNK):
        cols = slice(chunk * GATE_CHUNK, (chunk + 1) * GATE_CHUNK)
        r = jax.nn.sigmoid(_block_diag_chunk_mm(xcb, wa_ref, chunk) + ba_ref[:, cols])
        ig = jax.nn.sigmoid(_block_diag_chunk_mm(xcb, wx_ref, chunk) + bx_ref[:, cols])
        log_a = r * log_a_scale[:, cols]
        a = jnp.exp(log_a)
        v = -jnp.tanh(log_a) * (1.0 + a * a)
        a_buf[:, cols] = a
        b_buf[:, cols] = jnp.where(v > 0.0, v * lax.rsqrt(v), 0.0) * (ig * xc[:, cols])
    gate_buf[...] = jax.nn.gelu(_mm(xn, win_ref[:, :D_RNN]))
    return xb_tail


def _lru_recur(x, state, wout_ref, a_buf, b_buf, gate_buf):
    local = _pos(b_buf, 0)
    decay = _pos(a_buf, 0)
    for j in range(1, CHUNK):
        rows = slice(j * SUBLANES, (j + 1) * SUBLANES)
        a_j = a_buf[rows, :]
        local = a_j * local + b_buf[rows, :]
        decay = a_j * decay
        b_buf[rows, :] = local
        a_buf[rows, :] = decay
    chunk_end = _scan_chunks(decay, local, state)
    chunk_entry = _from_previous_chunk(chunk_end, state)
    h = (b_buf[...].reshape(CHUNK, SUBLANES, D_RNN)
         + a_buf[...].reshape(CHUNK, SUBLANES, D_RNN) * chunk_entry[None]).reshape(MIX_TILE, D_RNN)

    y = (h * gate_buf[...]).astype(BF16)
    return x + _mm(y, wout_ref[...]), jnp.broadcast_to(chunk_end[SUBLANES - 1:, :], state.shape)


def _lru_kernel(x_ref, g_ref, win_ref, cw_ref, cb_ref, wa_ref, ba_ref, wx_ref, bx_ref, ap_ref, wout_ref, o_ref,
                xb_ext, a_buf, b_buf, gate_buf, tail_ref, state_ref):
    @pl.when(pl.program_id(1) == 0)
    def _():
        tail_ref[...] = jnp.zeros_like(tail_ref)
        state_ref[...] = jnp.zeros_like(state_ref)

    blocks = x_ref.shape[0] // MIX_TILE
    xb_tail = tail_ref[...]
    for r in range(blocks):
        rows = slice(r * MIX_TILE, (r + 1) * MIX_TILE)
        xb_tail = _lru_project(x_ref[rows, :], xb_tail, g_ref, win_ref, cw_ref, cb_ref, wa_ref, ba_ref, wx_ref, bx_ref,
                               ap_ref, xb_ext.at[r], a_buf.at[r], b_buf.at[r], gate_buf.at[r])
    tail_ref[...] = xb_tail
    state = state_ref[...]
    for r in range(blocks):
        rows = slice(r * MIX_TILE, (r + 1) * MIX_TILE)
        o_ref[rows, :], state = _lru_recur(x_ref[rows, :], state, wout_ref, a_buf.at[r], b_buf.at[r], gate_buf.at[r])
    state_ref[...] = state


def _window_sum(ext, win):
    cur = ext[(POOL_BOUNDARY - (win - 1)) * SUBLANES:, :]
    step = 1
    while step < win:
        cur = cur[step * SUBLANES:, :] + cur[:-step * SUBLANES, :]
        step *= 2
    return cur


def _pool_block(x, hn_tail, first_token, g_ref, w_ref, b_ref, sc_ref, ext, o_ref, rows):
    hn = _rms(x, g_ref[...])
    hn_tail = _fill_boundary(ext, hn_tail, hn, POOL_BOUNDARY)
    row = lax.broadcasted_iota(jnp.int32, (MIX_TILE, POOL_GROUP_DIM), 0)
    t = first_token + (row & (SUBLANES - 1)) * CHUNK + lax.shift_right_logical(row, SUBLANES.bit_length() - 1)
    for gi, win in enumerate(POOL_WINDOWS):
        cols = slice(gi * POOL_GROUP_DIM, (gi + 1) * POOL_GROUP_DIM)
        count = jnp.minimum(t + 1, win).astype(F32)
        u = (_window_sum(ext[:, cols], win) / count - hn[:, cols]).astype(BF16)
        y = _mm(u, w_ref[gi])
        o_ref[rows, cols] = x[:, cols] + (y + b_ref[:, cols]) * sc_ref[:, cols]
    return hn_tail


def _layer_spec(stacked, layer):
    zeros = (0,) * (stacked.ndim - 1)
    return pl.BlockSpec((None,) + stacked.shape[1:], lambda b, s: (layer,) + zeros, pipeline_mode=pl.Buffered(1))


def _ffn_weight_copies(hbm_refs, vmem_refs, sem):
    copies = []
    for c in range(D_FF // FF_CHUNK):
        span = pl.ds(c * FF_CHUNK, FF_CHUNK)
        copies.append([
            pltpu.make_async_copy(hbm_refs[0].at[0, :, span], vmem_refs[0].at[:, span], sem.at[0, c]),
            pltpu.make_async_copy(hbm_refs[1].at[0, :, span], vmem_refs[1].at[:, span], sem.at[1, c]),
            pltpu.make_async_copy(hbm_refs[2].at[0, span, :], vmem_refs[2].at[span, :], sem.at[2, c])])
    return copies


def _tiled_call(kernel, name, n_batch, seq, tiled_inputs, layer_inputs, scratch_shapes=(), convert=(), ffn_weights=()):
    tiles = seq // GRID_TILE
    steps = n_batch * tiles

    def row_spec(width, copy=0):
        base = copy * n_batch * tiles
        return pl.BlockSpec((GRID_TILE, width), lambda b, s: (base + b * tiles + s, 0))

    def slab_in(stack, layer):
        _, rows, cols = stack.shape
        return pl.BlockSpec((None, rows // steps, cols), lambda b, s: (layer, b * tiles + s, 0))

    def slab_out(stack):
        _, rows, cols = stack.shape
        return pl.BlockSpec((rows // steps, cols), lambda b, s: (b * tiles + s, 0))

    for stack, _ in convert:
        assert stack.shape[1] % (steps * BF16_SUBLANES) == 0, stack.shape
    n_tiled, n_layer, n_ffn, n_cv = len(tiled_inputs), len(layer_inputs), len(ffn_weights), len(convert)
    n_in = n_tiled + n_layer + n_ffn
    n_scratch = len(scratch_shapes)

    def body(*refs):
        ins, ffn_hbm, cv_in = refs[:n_tiled + n_layer], refs[n_tiled + n_layer:n_in], refs[n_in:n_in + n_cv]
        o_ref, cv_out = refs[n_in + n_cv], refs[n_in + n_cv + 1:n_in + 2 * n_cv + 1]
        scratch = refs[n_in + 2 * n_cv + 1:]
        for src, dst in zip(cv_in, cv_out):
            dst[...] = src[...].astype(BF16)
        if not ffn_weights:
            kernel(*ins, o_ref, *scratch)
            return
        ffn_vmem, sem = scratch[n_scratch:n_scratch + n_ffn], scratch[n_scratch + n_ffn]
        copies = _ffn_weight_copies(ffn_hbm, ffn_vmem, sem)
        args = (*ins[:n_tiled + 1], *ffn_vmem, *ins[n_tiled + 1:], o_ref, *scratch[:n_scratch])
        first_step = jnp.logical_and(pl.program_id(0) == 0, pl.program_id(1) == 0)

        def wait_chunk(c):
            for copy in copies[c]:
                copy.wait()

        @pl.when(first_step)
        def _():
            for chunk in copies:
                for copy in chunk:
                    copy.start()
            kernel(*args, wait_chunk=wait_chunk)

        @pl.when(jnp.logical_not(first_step))
        def _():
            kernel(*args, wait_chunk=None)

    ffn_scratch = [pltpu.VMEM(w.shape[1:], BF16) for w in ffn_weights]
    if ffn_weights:
        ffn_scratch.append(pltpu.SemaphoreType.DMA((n_ffn, D_FF // FF_CHUNK)))
    return pl.pallas_call(
        body,
        name=name,
        grid=(n_batch, tiles),
        in_specs=[row_spec(a.shape[1], k) for a, k in tiled_inputs] + [_layer_spec(a, l) for a, l in layer_inputs]
        + [pl.BlockSpec(memory_space=pl.ANY)] * n_ffn + [slab_in(a, l) for a, l in convert],
        out_specs=[row_spec(D_MODEL)] + [slab_out(a) for a, _ in convert],
        out_shape=[jax.ShapeDtypeStruct((n_batch * seq, D_MODEL), F32)]
        + [jax.ShapeDtypeStruct(a.shape[1:], BF16) for a, _ in convert],
        scratch_shapes=list(scratch_shapes) + ffn_scratch,
        compiler_params=pltpu.CompilerParams(
            dimension_semantics=("arbitrary", "arbitrary"), vmem_limit_bytes=VMEM_LIMIT_BYTES),
    )(*[a for a, _ in tiled_inputs], *[a for a, _ in layer_inputs], *ffn_weights, *[a for a, _ in convert])


def _rows_of(stacked):
    return stacked.reshape(stacked.shape[0], 1, stacked.shape[1])


def _block_diag(w):
    layers, heads, d, _ = w.shape
    n = heads * d
    col = jnp.arange(n)
    spread = (col[None, :] % d == jnp.arange(d)[:, None]).astype(BF16)
    tiled = jnp.einsum("lrj,jc->lrc", w.reshape(layers, n, d).astype(BF16), spread, preferred_element_type=F32)
    same_head = (col[:, None] // d) == (col[None, :] // d)
    return jnp.where(same_head[None], tiled, 0.0).astype(BF16)


def kernel(x, p, ffn1_norm, ffn1_w_gate, ffn1_w_up, ffn1_w_down, mix_norm, lru_w_in, lru_conv_w, lru_conv_b, lru_w_a, lru_b_a, lru_w_x, lru_b_x, lru_a_param, lru_w_out, pool_w, pool_b, pool_scale, ffn2_norm, ffn2_w_gate, ffn2_w_up, ffn2_w_down, ple_norm, ple_w_gate, ple_w_proj, final_norm):
    n_batch, seq, d_model = x.shape
    depth = p.shape[0]
    assert d_model == D_MODEL and seq % GRID_TILE == 0 and GRID_TILE % MIX_TILE == 0
    call = functools.partial(_tiled_call, n_batch=n_batch, seq=seq)
    one = lambda w: (w.reshape((1,) + w.shape), 0)
    ffn1_f32 = [ffn1_w_gate, ffn1_w_up, ffn1_w_down]
    ffn2_f32 = [ffn2_w_gate, ffn2_w_up, ffn2_w_down, ple_w_gate, ple_w_proj]
    pool_w_rows = pool_w.reshape(pool_w.shape[0], -1, POOL_GROUP_DIM)
    g1, g2, gm, gp = _rows_of(ffn1_norm), _rows_of(ffn2_norm), _rows_of(mix_norm), _rows_of(ple_norm)
    wa, wx = _block_diag(lru_w_a), _block_diag(lru_w_x)
    lru_rows = [_rows_of(v) for v in (lru_conv_b, lru_b_a, lru_b_x, lru_a_param)]
    pool_rows = [_rows_of(pool_b), _rows_of(pool_scale)]
    final_g = final_norm.reshape(1, 1, D_MODEL)
    p_rows = p.reshape(depth * n_batch * seq, PLE_DIM)
    blocks = GRID_TILE // MIX_TILE

    h = x.reshape(n_batch * seq, D_MODEL)
    ffn1_bf = [w[0].astype(BF16) for w in ffn1_f32]
    pool_bf = None
    for i in range(depth):
        j = i // 2
        ffn1_w = [w.reshape((1,) + w.shape) for w in ffn1_bf]
        if i % 2 == 0:
            h, *cast = call(functools.partial(_ffn_kernel, interleave_input=(i == 0)), "ffn1", tiled_inputs=[(h, 0)],
                            layer_inputs=[(g1, i)], ffn_weights=ffn1_w,
                            convert=[(w, i) for w in ffn2_f32] + [(lru_w_in, j), (lru_w_out, j)])
            *ffn2_bf, w_in, w_out = cast
            cb, ba, bx, ap = lru_rows
            h, = call(_lru_kernel, "lru", tiled_inputs=[(h, 0)],
                      layer_inputs=[(gm, i), one(w_in), (lru_conv_w, j), (cb, j), (wa, j), (ba, j), (wx, j), (bx, j),
                                    (ap, j), one(w_out)],
                      scratch_shapes=[pltpu.VMEM((blocks, CONV_BOUNDARY * SUBLANES + MIX_TILE, D_RNN), F32),
                                      pltpu.VMEM((blocks, MIX_TILE, D_RNN), F32),
                                      pltpu.VMEM((blocks, MIX_TILE, D_RNN), F32),
                                      pltpu.VMEM((blocks, MIX_TILE, D_RNN), F32),
                                      pltpu.VMEM((CONV_BOUNDARY * SUBLANES, D_RNN), F32),
                                      pltpu.VMEM((SUBLANES, D_RNN), F32)])
        else:
            h, *ffn2_bf = call(_ffn_pool_kernel, "ffn1_pool", tiled_inputs=[(h, 0)],
                               layer_inputs=[(g1, i), (gm, i), one(pool_bf.reshape(pool_w.shape[1:]))]
                               + [(v, j) for v in pool_rows], ffn_weights=ffn1_w,
                               scratch_shapes=[pltpu.VMEM((blocks, POOL_BOUNDARY * SUBLANES + MIX_TILE, D_MODEL), F32),
                                               pltpu.VMEM((POOL_BOUNDARY * SUBLANES, D_MODEL), F32)],
                               convert=[(w, i) for w in ffn2_f32])
        last = i == depth - 1
        wg, wu, wd, wpg, wpp = ffn2_bf
        next_pool = [(pool_w_rows, (i + 1) // 2)] if (i + 1) % 2 == 1 else []
        h, *cast = call(functools.partial(_ffn_ple_kernel, final_norm=last), "ffn2_ple",
                        tiled_inputs=[(h, 0), (p_rows, i)],
                        layer_inputs=[(g2, i), (gp, i), one(wpg), one(wpp), (final_g, 0)],
                        ffn_weights=[w.reshape((1,) + w.shape) for w in (wg, wu, wd)],
                        convert=[] if last else [(w, i + 1) for w in ffn1_f32] + next_pool)
        if not last:
            ffn1_bf, pool_bf = cast[:len(ffn1_f32)], (cast[len(ffn1_f32)] if next_pool else None)
    return h.reshape(n_batch, seq, D_MODEL)
```

```python
import functools

import jax
import jax.numpy as jnp
from jax import lax
from jax.experimental import pallas as pl
from jax.experimental.pallas import tpu as pltpu

D_MODEL = 1024
D_FF = 2816
D_RNN = 1280
LRU_HEADS = 16
LRU_HEAD_DIM = 80
CONV_WIDTH = 4
LRU_C = 8.0
POOL_WINDOWS = (2, 4, 8, 16)
POOL_GROUP_DIM = 256
PLE_DIM = 256
RMS_EPS = 1e-6

SUBLANES = 8
LANES = 128
BF16_SUBLANES = 16
GATE_CHUNK = 256
GATE_WINDOW = 512
FF_CHUNK = 256
MIX_TILE = 512
CHUNK = MIX_TILE // SUBLANES
GRID_TILE = 1024
CONV_BOUNDARY = CONV_WIDTH - 1
CONV_EXT_ROWS = (CONV_BOUNDARY + 1) * SUBLANES + MIX_TILE
POOL_BOUNDARY = max(POOL_WINDOWS) - 1
VMEM_LIMIT_BYTES = 56 * 1024 * 1024

F32 = jnp.float32
BF16 = jnp.bfloat16


def _rms(x, g):
    return x * lax.rsqrt(jnp.mean(x * x, axis=-1, keepdims=True) + RMS_EPS) * g


def _mm(a, w):
    return jnp.dot(a, w, preferred_element_type=F32)


def _interleave(x):
    w = x.shape[1]
    return jnp.swapaxes(x.reshape(SUBLANES, CHUNK, w), 0, 1).reshape(MIX_TILE, w)


def _deinterleave(x):
    w = x.shape[1]
    return jnp.swapaxes(x.reshape(CHUNK, SUBLANES, w), 0, 1).reshape(MIX_TILE, w)


def _pos(x, j):
    return x[j * SUBLANES:(j + 1) * SUBLANES, :]


def _from_previous_chunk(cur, prev_tile):
    chunk = lax.broadcasted_iota(jnp.int32, cur.shape, 0)
    return pltpu.roll(jnp.where(chunk == SUBLANES - 1, prev_tile, cur), 1, axis=0)


def _fill_boundary(ext_ref, prev_tail, cur, n):
    for i in range(n):
        ext_ref[i * SUBLANES:(i + 1) * SUBLANES, :] = _from_previous_chunk(_pos(cur, CHUNK - n + i), _pos(prev_tail, i))
    ext_ref[n * SUBLANES:n * SUBLANES + cur.shape[0], :] = cur
    return cur[(CHUNK - n) * SUBLANES:, :]


def _swiglu_residual(x, g_ref, wg_ref, wu_ref, wd_ref):
    xn = _rms(x, g_ref[...]).astype(BF16)
    y = None
    for c in range(D_FF // FF_CHUNK):
        cols = slice(c * FF_CHUNK, (c + 1) * FF_CHUNK)
        gate = _mm(xn, wg_ref[:, cols])
        up = _mm(xn, wu_ref[:, cols])
        act = (gate * jax.nn.sigmoid(gate) * up).astype(BF16)
        part = _mm(act, wd_ref[cols, :])
        y = part if y is None else y + part
    return x + 0.5 * y


def _ffn_kernel(x_ref, g_ref, wg_ref, wu_ref, wd_ref, o_ref, *, interleave_input):
    for r in range(x_ref.shape[0] // MIX_TILE):
        rows = slice(r * MIX_TILE, (r + 1) * MIX_TILE)
        x = x_ref[rows, :]
        if interleave_input:
            x = _interleave(x)
        o_ref[rows, :] = _swiglu_residual(x, g_ref, wg_ref, wu_ref, wd_ref)


def _ffn_pool_kernel(x_ref, g_ref, wg_ref, wu_ref, wd_ref, mg_ref, pw_ref, pb_ref, ps_ref, o_ref, hn_ext, tail_ref):
    seq_tile = pl.program_id(1)

    @pl.when(seq_tile == 0)
    def _():
        tail_ref[...] = jnp.zeros_like(tail_ref)

    blocks = [slice(r * MIX_TILE, (r + 1) * MIX_TILE) for r in range(x_ref.shape[0] // MIX_TILE)]
    hs = [_swiglu_residual(x_ref[rows, :], g_ref, wg_ref, wu_ref, wd_ref) for rows in blocks]
    hn_tail = tail_ref[...]
    for r, (rows, h) in enumerate(zip(blocks, hs)):
        first_token = seq_tile * x_ref.shape[0] + r * MIX_TILE
        hn_tail = _pool_block(h, hn_tail, first_token, mg_ref, pw_ref, pb_ref, ps_ref, hn_ext.at[r], o_ref, rows)
    tail_ref[...] = hn_tail


def _ffn_ple_kernel(x_ref, p_ref, g_ref, wg_ref, wu_ref, wd_ref, pg_ref, wpg_ref, wpp_ref, fg_ref, o_ref, *,
                    final_norm):
    blocks = [slice(r * MIX_TILE, (r + 1) * MIX_TILE) for r in range(x_ref.shape[0] // MIX_TILE)]
    hs = [_swiglu_residual(x_ref[rows, :], g_ref, wg_ref, wu_ref, wd_ref) for rows in blocks]
    projs = [_mm(_interleave(p_ref[rows, :]).astype(BF16), wpp_ref[...]) for rows in blocks]
    for rows, h, proj in zip(blocks, hs, projs):
        gate = jax.nn.sigmoid(_mm(_rms(h, pg_ref[...]).astype(BF16), wpg_ref[...]))
        h = h + gate * proj
        if final_norm:
            h = _deinterleave(_rms(h, fg_ref[...]))
        o_ref[rows, :] = h


def _gate_window_start(chunk):
    lo = chunk * GATE_CHUNK // LRU_HEAD_DIM * LRU_HEAD_DIM
    hi = -(-(chunk + 1) * GATE_CHUNK // LRU_HEAD_DIM) * LRU_HEAD_DIM
    start = min(lo // LANES * LANES, D_RNN - GATE_WINDOW)
    assert start <= lo and hi <= start + GATE_WINDOW
    return start


def _block_diag_chunk_mm(a, w_ref, chunk):
    k0 = _gate_window_start(chunk)
    cols = slice(chunk * GATE_CHUNK, (chunk + 1) * GATE_CHUNK)
    return _mm(a[:, k0:k0 + GATE_WINDOW], w_ref[k0:k0 + GATE_WINDOW, cols])


def _scan_chunks(decay, inp, state):
    chunk = lax.broadcasted_iota(jnp.int32, decay.shape, 0)
    inp = inp + jnp.where(chunk == 0, decay * state, 0.0)
    step = 1
    while step < SUBLANES:
        seen = chunk >= step
        inp = decay * jnp.where(seen, pltpu.roll(inp, step, axis=0), 0.0) + inp
        decay = decay * jnp.where(seen, pltpu.roll(decay, step, axis=0), 1.0)
        step *= 2
    return inp


def _lru_project(x, xb_tail, g_ref, win_ref, cw_ref, cb_ref, wa_ref, ba_ref, wx_ref, bx_ref, ap_ref,
                 xb_ext, a_buf, b_buf, gate_buf):
    xn = _rms(x, g_ref[...]).astype(BF16)
    xb = _mm(xn, win_ref[:, D_RNN:])
    xb_tail = _fill_boundary(xb_ext, xb_tail, xb, CONV_BOUNDARY)
    xc = cb_ref[...] + cw_ref[CONV_WIDTH - 1:CONV_WIDTH, :] * xb
    for m in range(1, CONV_WIDTH):
        start = (CONV_BOUNDARY - m) * SUBLANES
        xc = xc + cw_ref[CONV_WIDTH - 1 - m:CONV_WIDTH - m, :] * xb_ext[start:start + MIX_TILE, :]
    xcb = xc.astype(BF16)
    log_a_scale = (-LRU_C) * jax.nn.softplus(-ap_ref[...])
    for chunk in range(D_RNN // GATE_CHUNK):
        cols = slice(chunk * GATE_CHUNK, (chunk + 1) * GATE_CHUNK)
        r = jax.nn.sigmoid(_block_diag_chunk_mm(xcb, wa_ref, chunk) + ba_ref[:, cols])
        ig = jax.nn.sigmoid(_block_diag_chunk_mm(xcb, wx_ref, chunk) + bx_ref[:, cols])
        log_a = r * log_a_scale[:, cols]
        a = jnp.exp(log_a)
        v = -jnp.tanh(log_a) * (1.0 + a * a)
        a_buf[:, cols] = a
        b_buf[:, cols] = jnp.where(v > 0.0, v * lax.rsqrt(v), 0.0) * (ig * xc[:, cols])
    gate_buf[...] = jax.nn.gelu(_mm(xn, win_ref[:, :D_RNN]))
    return xb_tail


def _lru_recur(x, state, wout_ref, a_buf, b_buf, gate_buf):
    local = _pos(b_buf, 0)
    decay = _pos(a_buf, 0)
    for j in range(1, CHUNK):
        a_j = _pos(a_buf, j)
        local = a_j * local + _pos(b_buf, j)
        decay = a_j * decay
    chunk_end = _scan_chunks(decay, local, state)
    h = _from_previous_chunk(chunk_end, state)
    for j in range(CHUNK):
        rows = slice(j * SUBLANES, (j + 1) * SUBLANES)
        h = a_buf[rows, :] * h + b_buf[rows, :]
        b_buf[rows, :] = h * gate_buf[rows, :]
    y = b_buf[...].astype(BF16)
    return x + _mm(y, wout_ref[...]), jnp.broadcast_to(chunk_end[SUBLANES - 1:, :], state.shape)


def _lru_kernel(x_ref, g_ref, win_ref, cw_ref, cb_ref, wa_ref, ba_ref, wx_ref, bx_ref, ap_ref, wout_ref, o_ref,
                xb_ext, a_buf, b_buf, gate_buf, tail_ref, state_ref):
    @pl.when(pl.program_id(1) == 0)
    def _():
        tail_ref[...] = jnp.zeros_like(tail_ref)
        state_ref[...] = jnp.zeros_like(state_ref)

    blocks = x_ref.shape[0] // MIX_TILE
    xb_tail = tail_ref[...]
    for r in range(blocks):
        rows = slice(r * MIX_TILE, (r + 1) * MIX_TILE)
        xb_tail = _lru_project(x_ref[rows, :], xb_tail, g_ref, win_ref, cw_ref, cb_ref, wa_ref, ba_ref, wx_ref, bx_ref,
                               ap_ref, xb_ext.at[r], a_buf.at[r], b_buf.at[r], gate_buf.at[r])
    tail_ref[...] = xb_tail
    state = state_ref[...]
    for r in range(blocks):
        rows = slice(r * MIX_TILE, (r + 1) * MIX_TILE)
        o_ref[rows, :], state = _lru_recur(x_ref[rows, :], state, wout_ref, a_buf.at[r], b_buf.at[r], gate_buf.at[r])
    state_ref[...] = state


def _window_sum(ext, win):
    cur = ext[(POOL_BOUNDARY - (win - 1)) * SUBLANES:, :]
    step = 1
    while step < win:
        cur = cur[step * SUBLANES:, :] + cur[:-step * SUBLANES, :]
        step *= 2
    return cur


def _pool_block(x, hn_tail, first_token, g_ref, w_ref, b_ref, sc_ref, ext, o_ref, rows):
    hn = _rms(x, g_ref[...])
    hn_tail = _fill_boundary(ext, hn_tail, hn, POOL_BOUNDARY)
    row = lax.broadcasted_iota(jnp.int32, (MIX_TILE, POOL_GROUP_DIM), 0)
    t = first_token + (row & (SUBLANES - 1)) * CHUNK + lax.shift_right_logical(row, SUBLANES.bit_length() - 1)
    for gi, win in enumerate(POOL_WINDOWS):
        cols = slice(gi * POOL_GROUP_DIM, (gi + 1) * POOL_GROUP_DIM)
        count = jnp.minimum(t + 1, win).astype(F32)
        u = (_window_sum(ext[:, cols], win) / count - hn[:, cols]).astype(BF16)
        y = _mm(u, w_ref[gi])
        o_ref[rows, cols] = x[:, cols] + (y + b_ref[:, cols]) * sc_ref[:, cols]
    return hn_tail


def _layer_spec(stacked, layer):
    zeros = (0,) * (stacked.ndim - 1)
    return pl.BlockSpec((None,) + stacked.shape[1:], lambda b, s: (layer,) + zeros, pipeline_mode=pl.Buffered(1))


def _tiled_call(kernel, name, n_batch, seq, tiled_inputs, layer_inputs, scratch_shapes=(), convert=()):
    tiles = seq // GRID_TILE
    steps = n_batch * tiles

    def row_spec(width, copy=0):
        base = copy * n_batch * tiles
        return pl.BlockSpec((GRID_TILE, width), lambda b, s: (base + b * tiles + s, 0))

    def slab_in(stack, layer):
        _, rows, cols = stack.shape
        return pl.BlockSpec((None, rows // steps, cols), lambda b, s: (layer, b * tiles + s, 0))

    def slab_out(stack):
        _, rows, cols = stack.shape
        return pl.BlockSpec((rows // steps, cols), lambda b, s: (b * tiles + s, 0))

    for stack, _ in convert:
        assert stack.shape[1] % (steps * BF16_SUBLANES) == 0, stack.shape
    n_in = len(tiled_inputs) + len(layer_inputs)
    n_cv = len(convert)

    def body(*refs):
        ins, cv_in = refs[:n_in], refs[n_in:n_in + n_cv]
        o_ref, cv_out = refs[n_in + n_cv], refs[n_in + n_cv + 1:n_in + 2 * n_cv + 1]
        for src, dst in zip(cv_in, cv_out):
            dst[...] = src[...].astype(BF16)
        kernel(*ins, o_ref, *refs[n_in + 2 * n_cv + 1:])

    return pl.pallas_call(
        body,
        name=name,
        grid=(n_batch, tiles),
        in_specs=[row_spec(a.shape[1], k) for a, k in tiled_inputs] + [_layer_spec(a, l) for a, l in layer_inputs]
        + [slab_in(a, l) for a, l in convert],
        out_specs=[row_spec(D_MODEL)] + [slab_out(a) for a, _ in convert],
        out_shape=[jax.ShapeDtypeStruct((n_batch * seq, D_MODEL), F32)]
        + [jax.ShapeDtypeStruct(a.shape[1:], BF16) for a, _ in convert],
        scratch_shapes=list(scratch_shapes),
        compiler_params=pltpu.CompilerParams(
            dimension_semantics=("arbitrary", "arbitrary"), vmem_limit_bytes=VMEM_LIMIT_BYTES),
    )(*[a for a, _ in tiled_inputs], *[a for a, _ in layer_inputs], *[a for a, _ in convert])


def _rows_of(stacked):
    return stacked.reshape(stacked.shape[0], 1, stacked.shape[1])


def _block_diag(w):
    layers, heads, d, _ = w.shape
    n = heads * d
    col = jnp.arange(n)
    spread = (col[None, :] % d == jnp.arange(d)[:, None]).astype(BF16)
    tiled = jnp.einsum("lrj,jc->lrc", w.reshape(layers, n, d).astype(BF16), spread, preferred_element_type=F32)
    same_head = (col[:, None] // d) == (col[None, :] // d)
    return jnp.where(same_head[None], tiled, 0.0).astype(BF16)


def kernel(x, p, ffn1_norm, ffn1_w_gate, ffn1_w_up, ffn1_w_down, mix_norm, lru_w_in, lru_conv_w, lru_conv_b, lru_w_a, lru_b_a, lru_w_x, lru_b_x, lru_a_param, lru_w_out, pool_w, pool_b, pool_scale, ffn2_norm, ffn2_w_gate, ffn2_w_up, ffn2_w_down, ple_norm, ple_w_gate, ple_w_proj, final_norm):
    n_batch, seq, d_model = x.shape
    depth = p.shape[0]
    assert d_model == D_MODEL and seq % GRID_TILE == 0 and GRID_TILE % MIX_TILE == 0
    call = functools.partial(_tiled_call, n_batch=n_batch, seq=seq)
    one = lambda w: (w.reshape((1,) + w.shape), 0)
    ffn1_f32 = [ffn1_w_gate, ffn1_w_up, ffn1_w_down]
    ffn2_f32 = [ffn2_w_gate, ffn2_w_up, ffn2_w_down, ple_w_gate, ple_w_proj]
    pool_w_rows = pool_w.reshape(pool_w.shape[0], -1, POOL_GROUP_DIM)
    g1, g2, gm, gp = _rows_of(ffn1_norm), _rows_of(ffn2_norm), _rows_of(mix_norm), _rows_of(ple_norm)
    wa, wx = _block_diag(lru_w_a), _block_diag(lru_w_x)
    lru_rows = [_rows_of(v) for v in (lru_conv_b, lru_b_a, lru_b_x, lru_a_param)]
    pool_rows = [_rows_of(pool_b), _rows_of(pool_scale)]
    final_g = final_norm.reshape(1, 1, D_MODEL)
    p_rows = p.reshape(depth * n_batch * seq, PLE_DIM)
    blocks = GRID_TILE // MIX_TILE

    h = x.reshape(n_batch * seq, D_MODEL)
    ffn1_bf = [w[0].astype(BF16) for w in ffn1_f32]
    pool_bf = None
    for i in range(depth):
        j = i // 2
        ffn1_layers = [(g1, i)] + [one(w) for w in ffn1_bf]
        if i % 2 == 0:
            h, *cast = call(functools.partial(_ffn_kernel, interleave_input=(i == 0)), "ffn1", tiled_inputs=[(h, 0)],
                            layer_inputs=ffn1_layers,
                            convert=[(w, i) for w in ffn2_f32] + [(lru_w_in, j), (lru_w_out, j)])
            *ffn2_bf, w_in, w_out = cast
            cb, ba, bx, ap = lru_rows
            h, = call(_lru_kernel, "lru", tiled_inputs=[(h, 0)],
                      layer_inputs=[(gm, i), one(w_in), (lru_conv_w, j), (cb, j), (wa, j), (ba, j), (wx, j), (bx, j),
                                    (ap, j), one(w_out)],
                      scratch_shapes=[pltpu.VMEM((blocks, CONV_EXT_ROWS, D_RNN), F32),
                                      pltpu.VMEM((blocks, MIX_TILE, D_RNN), F32),
                                      pltpu.VMEM((blocks, MIX_TILE, D_RNN), F32),
                                      pltpu.VMEM((blocks, MIX_TILE, D_RNN), F32),
                                      pltpu.VMEM((CONV_BOUNDARY * SUBLANES, D_RNN), F32),
                                      pltpu.VMEM((SUBLANES, D_RNN), F32)])
        else:
            h, *ffn2_bf = call(_ffn_pool_kernel, "ffn1_pool", tiled_inputs=[(h, 0)],
                               layer_inputs=ffn1_layers + [(gm, i), one(pool_bf.reshape(pool_w.shape[1:]))]
                               + [(v, j) for v in pool_rows],
                               scratch_shapes=[pltpu.VMEM((blocks, POOL_BOUNDARY * SUBLANES + MIX_TILE, D_MODEL), F32),
                                               pltpu.VMEM((POOL_BOUNDARY * SUBLANES, D_MODEL), F32)],
                               convert=[(w, i) for w in ffn2_f32])
        last = i == depth - 1
        wg, wu, wd, wpg, wpp = ffn2_bf
        next_pool = [(pool_w_rows, (i + 1) // 2)] if (i + 1) % 2 == 1 else []
        h, *cast = call(functools.partial(_ffn_ple_kernel, final_norm=last), "ffn2_ple",
                        tiled_inputs=[(h, 0), (p_rows, i)],
                        layer_inputs=[(g2, i), one(wg), one(wu), one(wd), (gp, i), one(wpg), one(wpp), (final_g, 0)],
                        convert=[] if last else [(w, i + 1) for w in ffn1_f32] + next_pool)
        if not last:
            ffn1_bf, pool_bf = cast[:len(ffn1_f32)], (cast[len(ffn1_f32)] if next_pool else None)
    return h.reshape(n_batch, seq, D_MODEL)
```
